```python
import math
import jax, jax.numpy as jnp
from jax import lax
import numpy as np

D_MODEL = 1024
BATCH = 32
SEQ = 256
DEPTH = 1
DEC_BATCH = 4
DEC_SEQ = 2048
PAST_LEN = 256

GRID_W = 64
N_HEADS = 16
HEAD_DIM = D_MODEL // N_HEADS
D_RWKV = N_HEADS * HEAD_DIM
D_CONV = D_MODEL
CONV_WIDTH = 31
LORA_W = 64
LORA_A = 64
LORA_G = 128
D_FF = 4 * D_MODEL
N_MOD = 9
EPS = 1e-6
LN_EPS = 1e-5
GN_EPS = 64e-5
DECAY_SCALE = 0.606531
POS_MAX_PERIOD = 10000.0
N_RWKV_COLS = 3 * D_RWKV + 2 * LORA_W + 2 * LORA_A + LORA_G
N_IN_COLS = N_RWKV_COLS + 2 * D_CONV + 2 * D_MODEL
RWKV_SPLITS = [D_RWKV, 2 * D_RWKV, 3 * D_RWKV, 3 * D_RWKV + LORA_W, 3 * D_RWKV + 2 * LORA_W,
               3 * D_RWKV + 2 * LORA_W + LORA_A, 3 * D_RWKV + 2 * LORA_W + 2 * LORA_A]

kernel_name = 'hybrid_rwkv7_conformer_flow_step'


def rms_norm(x, g):
    xf = x.astype(jnp.float32)
    y = xf * lax.rsqrt(jnp.mean(xf * xf, axis=-1, keepdims=True) + EPS)
    return (y * g.astype(jnp.float32)).astype(x.dtype)


def layer_norm(x, g, b):
    xf = x.astype(jnp.float32)
    mean = jnp.mean(xf, axis=-1, keepdims=True)
    var = jnp.mean(jnp.square(xf - mean), axis=-1, keepdims=True)
    y = (xf - mean) * lax.rsqrt(var + LN_EPS) * g.astype(jnp.float32) + b.astype(jnp.float32)
    return y.astype(x.dtype)


def modulate(x, gain, shift, scale):
    return rms_norm(x, gain) * (1 + scale) + shift


def swiglu_ffn(h, w_in, w_out):
    gate, up = jnp.split(h @ w_in, 2, axis=-1)
    return (jax.nn.silu(gate) * up) @ w_out


def grid_pos_embed(rows):
    row, col = jnp.meshgrid(jnp.arange(rows, dtype=jnp.float32),
                            jnp.arange(GRID_W, dtype=jnp.float32), indexing='ij')
    n_freq = D_MODEL // 4
    freqs = jnp.exp(-math.log(POS_MAX_PERIOD) * jnp.arange(n_freq, dtype=jnp.float32) / n_freq)
    ang_r = row.reshape(-1, 1) * freqs
    ang_c = col.reshape(-1, 1) * freqs
    return jnp.concatenate([jnp.sin(ang_r), jnp.cos(ang_r), jnp.sin(ang_c), jnp.cos(ang_c)], axis=-1)


def centred_shift(u):
    pad = jnp.pad(u, ((0, 0), (1, 1), (0, 0)))
    return 0.5 * (pad[:, :-2] + pad[:, 2:])


def depthwise_conv(u, w, b):
    out = lax.conv_general_dilated(
        u, w[:, None, :].astype(u.dtype), window_strides=(1,),
        padding=[(CONV_WIDTH // 2, CONV_WIDTH // 2)],
        dimension_numbers=('NWC', 'WIO', 'NWC'), feature_group_count=u.shape[-1])
    return out + b


def wkv7_scan(r, w, k, v, kk, a, s0, reverse):
    def step(s, inp):
        r_t, w_t, k_t, v_t, kk_t, a_t = inp
        sa = jnp.einsum('bhij,bhj->bhi', s, -kk_t)
        s = (s * w_t[:, :, None, :] + sa[..., :, None] * (kk_t * a_t)[:, :, None, :]
             + v_t[..., :, None] * k_t[:, :, None, :])
        y = jnp.einsum('bhij,bhj->bhi', s, r_t)
        return s, y
    xs = tuple(jnp.moveaxis(t.astype(jnp.float32), 1, 0) for t in (r, w, k, v, kk, a))
    s_final, ys = lax.scan(step, s0.astype(jnp.float32), xs, reverse=reverse)
    return jnp.moveaxis(ys, 0, 1), s_final


def rwkv7_branch(u_rwkv, s_fwd0, s_bwd0, w0, w_up, a0, a_up, g_up, k_k, k_a, r_k, lnx_w, lnx_b):
    B, T, _ = u_rwkv.shape
    r, k, v, wd_f, wd_b, ad_f, ad_b, gd = jnp.split(u_rwkv, RWKV_SPLITS, axis=-1)
    heads = lambda t: t.reshape(B, T, N_HEADS, HEAD_DIM)
    rh, vh = heads(r), heads(v)
    kk = heads(k * k_k).astype(jnp.float32)
    kk = kk * lax.rsqrt(jnp.sum(kk * kk, axis=-1, keepdims=True) + 1e-12)
    y_sum = 0.0
    bonus = 0.0
    finals = []
    for d, (wd, ad, s0, rev) in enumerate(((wd_f, ad_f, s_fwd0, False), (wd_b, ad_b, s_bwd0, True))):
        decay = jnp.exp(-DECAY_SCALE * jax.nn.sigmoid(w0[d] + jnp.tanh(wd) @ w_up[d]))
        a = jax.nn.sigmoid(a0[d] + ad @ a_up[d])
        k_d = heads(k * (1 + (a - 1) * k_a))
        y_d, s_fin = wkv7_scan(rh, heads(decay), k_d, vh, kk, heads(a), s0, rev)
        y_sum = y_sum + y_d
        bonus = bonus + jnp.sum((rh * k_d * r_k).astype(jnp.float32), axis=-1, keepdims=True) * vh.astype(jnp.float32)
        finals.append(s_fin)
    mean = jnp.mean(y_sum, axis=-1, keepdims=True)
    var = jnp.mean(jnp.square(y_sum - mean), axis=-1, keepdims=True)
    y = ((y_sum - mean) * lax.rsqrt(var + GN_EPS) * lnx_w.reshape(N_HEADS, HEAD_DIM).astype(jnp.float32)
         + lnx_b.reshape(N_HEADS, HEAD_DIM).astype(jnp.float32))
    y = (y + bonus).reshape(B, T, D_RWKV)
    g = jax.nn.sigmoid(gd) @ g_up
    return (y * g.astype(jnp.float32)).astype(u_rwkv.dtype), finals[0], finals[1]


def conv_branch(u_glu, dw, db, ln_w, ln_b):
    a, b = jnp.split(u_glu, 2, axis=-1)
    z = a * jax.nn.sigmoid(b)
    z = depthwise_conv(z, dw, db)
    z = layer_norm(z, ln_w, ln_b)
    return jax.nn.silu(z)


def trunk_layer(x, cond, s_fwd0, s_bwd0, w_mod, b_mod, norm_ffn1, ffn1_w_in, ffn1_w_out, norm_mix,
                mix_w_in, rwkv_mu, rwkv_w0, rwkv_w_up, rwkv_a0, rwkv_a_up, rwkv_g_up, rwkv_k_k,
                rwkv_k_a, rwkv_r_k, rwkv_lnx_w, rwkv_lnx_b, conv_dw, conv_db, conv_ln_w, conv_ln_b,
                mix_w_out, norm_ffn2, ffn2_w_in, ffn2_w_out):
    mod = jax.nn.silu(cond) @ w_mod + b_mod
    sh1, sc1, g1, sh2, sc2, g2, sh3, sc3, g3 = jnp.split(mod[..., None, :], N_MOD, axis=-1)
    x = x + g1 * (0.5 * swiglu_ffn(modulate(x, norm_ffn1, sh1, sc1), ffn1_w_in, ffn1_w_out))
    h = modulate(x, norm_mix, sh2, sc2)
    u = h @ mix_w_in
    u_rwkv = u[..., :N_RWKV_COLS]
    u_rwkv = u_rwkv + rwkv_mu * (centred_shift(u_rwkv) - u_rwkv)
    y_a, s_fwd, s_bwd = rwkv7_branch(u_rwkv, s_fwd0, s_bwd0, rwkv_w0, rwkv_w_up, rwkv_a0, rwkv_a_up,
                                     rwkv_g_up, rwkv_k_k, rwkv_k_a, rwkv_r_k, rwkv_lnx_w, rwkv_lnx_b)
    y_b = conv_branch(u[..., N_RWKV_COLS:N_RWKV_COLS + 2 * D_CONV], conv_dw, conv_db, conv_ln_w, conv_ln_b)
    gate_a, gate_b = jnp.split(jax.nn.sigmoid(u[..., N_RWKV_COLS + 2 * D_CONV:]), 2, axis=-1)
    x = x + g2 * ((gate_a * y_a + gate_b * y_b) @ mix_w_out)
    x = x + g3 * (0.5 * swiglu_ffn(modulate(x, norm_ffn2, sh3, sc3), ffn2_w_in, ffn2_w_out))
    return x, s_fwd, s_bwd


def setup_inputs(seed: int = 0) -> dict:
    key = jax.random.key(seed)
    ks = jax.random.split(key, 40)
    nrm = lambda i, shape, s: jax.random.normal(ks[i], shape, jnp.float32) * s
    L, D = DEPTH, D_MODEL
    st_shape = (DEC_BATCH, DEPTH, N_HEADS, HEAD_DIM, HEAD_DIM)
    return {
        'x_prompt': nrm(0, (BATCH, SEQ, D), 1.0),
        'x_sample': nrm(1, (DEC_BATCH, DEC_SEQ, D), 1.0),
        'state_rwkv_fwd': nrm(2, st_shape, 0.3),
        'state_rwkv_bwd': nrm(3, st_shape, 0.3),
        'c': nrm(4, (DEC_BATCH, D), 1.0),
        'c_ctx': nrm(5, (D,), 1.0),
        'w_mod': nrm(6, (L, D, N_MOD * D), D ** -0.5),
        'b_mod': nrm(7, (L, N_MOD * D), 0.01),
        'norm_ffn1': 1.0 + nrm(8, (L, D), 0.01),
        'ffn1_w_in': nrm(9, (L, D, 2 * D_FF), D ** -0.5),
        'ffn1_w_out': nrm(10, (L, D_FF, D), D_FF ** -0.5),
        'norm_mix': 1.0 + nrm(11, (L, D), 0.01),
        'mix_w_in': nrm(12, (L, D, N_IN_COLS), D ** -0.5),
        'rwkv_mu': jax.random.uniform(ks[13], (L, N_RWKV_COLS), jnp.float32),
        'rwkv_w0': nrm(14, (L, 2, D_RWKV), 1.0),
        'rwkv_w_up': nrm(15, (L, 2, LORA_W, D_RWKV), 0.1 * LORA_W ** -0.5),
        'rwkv_a0': nrm(16, (L, 2, D_RWKV), 0.5),
        'rwkv_a_up': nrm(17, (L, 2, LORA_A, D_RWKV), 0.1 * LORA_A ** -0.5),
        'rwkv_g_up': nrm(18, (L, LORA_G, D_RWKV), LORA_G ** -0.5),
        'rwkv_k_k': 0.85 + nrm(19, (L, D_RWKV), 0.1),
        'rwkv_k_a': 1.0 + nrm(20, (L, D_RWKV), 0.1),
        'rwkv_r_k': nrm(21, (L, N_HEADS, HEAD_DIM), 0.1),
        'rwkv_lnx_w': 1.0 + nrm(22, (L, D_RWKV), 0.01),
        'rwkv_lnx_b': nrm(23, (L, D_RWKV), 0.01),
        'conv_dw': nrm(24, (L, CONV_WIDTH, D_CONV), CONV_WIDTH ** -0.5),
        'conv_db': nrm(25, (L, D_CONV), 0.01),
        'conv_ln_w': 1.0 + nrm(26, (L, D_CONV), 0.01),
        'conv_ln_b': nrm(27, (L, D_CONV), 0.01),
        'mix_w_out': nrm(28, (L, D, D), D ** -0.5),
        'norm_ffn2': 1.0 + nrm(29, (L, D), 0.01),
        'ffn2_w_in': nrm(30, (L, D, 2 * D_FF), D ** -0.5),
        'ffn2_w_out': nrm(31, (L, D_FF, D), D_FF ** -0.5),
        'norm_final': 1.0 + nrm(32, (D,), 0.01),
    }


def reference(x_prompt, x_sample, state_rwkv_fwd, state_rwkv_bwd, c, c_ctx, w_mod, b_mod, norm_ffn1,
              ffn1_w_in, ffn1_w_out, norm_mix, mix_w_in, rwkv_mu, rwkv_w0, rwkv_w_up, rwkv_a0, rwkv_a_up,
              rwkv_g_up, rwkv_k_k, rwkv_k_a, rwkv_r_k, rwkv_lnx_w, rwkv_lnx_b, conv_dw, conv_db, conv_ln_w,
              conv_ln_b, mix_w_out, norm_ffn2, ffn2_w_in, ffn2_w_out, norm_final):
    rows = x_sample.shape[1] // GRID_W
    ctx = x_prompt
    lat = x_sample + grid_pos_embed(rows).astype(x_sample.dtype)[None]
    new_fwd, new_bwd = [], []
    for l in range(DEPTH):
        lw = (w_mod[l], b_mod[l], norm_ffn1[l], ffn1_w_in[l], ffn1_w_out[l], norm_mix[l], mix_w_in[l],
              rwkv_mu[l], rwkv_w0[l], rwkv_w_up[l], rwkv_a0[l], rwkv_a_up[l], rwkv_g_up[l], rwkv_k_k[l],
              rwkv_k_a[l], rwkv_r_k[l], rwkv_lnx_w[l], rwkv_lnx_b[l], conv_dw[l], conv_db[l], conv_ln_w[l],
              conv_ln_b[l], mix_w_out[l], norm_ffn2[l], ffn2_w_in[l], ffn2_w_out[l])
        zeros = jnp.zeros((ctx.shape[0], N_HEADS, HEAD_DIM, HEAD_DIM), jnp.float32)
        ctx, s_f, s_b = trunk_layer(ctx, c_ctx, zeros, zeros, *lw)
        new_fwd.append(s_f)
        new_bwd.append(s_b)
        lat, _, _ = trunk_layer(lat, c, state_rwkv_fwd[:, l], state_rwkv_bwd[:, l], *lw)
    y_prompt = rms_norm(ctx, norm_final)
    y_sample = rms_norm(lat, norm_final)
    new_state_rwkv_fwd = jnp.stack(new_fwd, axis=1).astype(x_prompt.dtype)
    new_state_rwkv_bwd = jnp.stack(new_bwd, axis=1).astype(x_prompt.dtype)
    return (y_prompt, y_sample, new_state_rwkv_fwd, new_state_rwkv_bwd)
```

```python
import functools
import math

import jax
import jax.numpy as jnp
from jax import lax
from jax.experimental import pallas as pl
from jax.experimental.pallas import tpu as pltpu

F32 = jnp.float32
BF16 = jnp.bfloat16

D = 1024
N_HEADS = 16
HEAD_DIM = 64
D_FF = 4 * D
N_MOD = 9
CONV_WIDTH = 31
CONV_HALF = CONV_WIDTH // 2
LORA_W = 64
LORA_A = 64
LORA_G = 128
N_RWKV_COLS = 3 * D + 2 * LORA_W + 2 * LORA_A + LORA_G
GRID_W = 64
EPS = 1e-6
LN_EPS = 1e-5
GN_EPS = 64e-5
DECAY_SCALE = 0.606531
POS_MAX_PERIOD = 10000.0

CTX_B, CTX_T = 32, 256
LAT_B, LAT_T = 4, 2048
N_CTX = CTX_B * CTX_T
N_TOK = N_CTX + LAT_B * LAT_T
SS_LEN = 2048
N_SS = N_TOK // SS_LEN
N_CTX_SS = N_CTX // SS_LEN

CHUNK = 64
HEAD_GROUP = 4
CTX_CHUNKS = CTX_T // CHUNK
LAT_CHUNKS = LAT_T // CHUNK
CTX_STEPS = CTX_B * CTX_CHUNKS
LAT_STEPS = LAT_B * LAT_CHUNKS
STAT_W = 256

VMEM_LIMIT = 56 * 1024 * 1024


def _cparams(sem):
    return pltpu.CompilerParams(dimension_semantics=sem, vmem_limit_bytes=VMEM_LIMIT)


def _sigmoid(x):
    return 1.0 / (1.0 + jnp.exp(-x))


def _dot(a, b):
    return jnp.dot(a, b, preferred_element_type=F32)


def _split_bf16(x):
    hi = x.astype(BF16)
    lo = (x - hi.astype(F32)).astype(BF16)
    return hi, lo


def _head_stat(x, bd):
    groups = D // STAT_W
    stacked = jnp.concatenate([x[:, g * STAT_W:(g + 1) * STAT_W] for g in range(groups)], axis=0)
    hi, lo = _split_bf16(stacked)
    s = _dot(hi, bd) + _dot(lo, bd)
    rows = x.shape[0]
    return jnp.concatenate([s[g * rows:(g + 1) * rows] for g in range(groups)], axis=1)


def _mod_kernel(cond_ref, w_ref, b_ref, o_ref):
    c = cond_ref[...]
    s = (c * _sigmoid(c)).astype(BF16)
    o_ref[...] = _dot(s, w_ref[...].astype(BF16)) + b_ref[...]


def _mod_call(cond, w_mod, b_mod):
    tn = 1152
    n = N_MOD * D
    return pl.pallas_call(
        _mod_kernel,
        out_shape=jax.ShapeDtypeStruct((8, n), F32),
        grid=(n // tn,),
        in_specs=[
            pl.BlockSpec((8, D), lambda j: (0, 0)),
            pl.BlockSpec((D, tn), lambda j: (0, j)),
            pl.BlockSpec((1, tn), lambda j: (0, j)),
        ],
        out_specs=pl.BlockSpec((8, tn), lambda j: (0, j)),
        compiler_params=_cparams(("arbitrary",)),
        name="mod",
    )(cond, w_mod, b_mod)


def _rms(x, gain):
    return x * lax.rsqrt(jnp.mean(x * x, axis=-1, keepdims=True) + EPS) * gain


def _ffn_kernel(x_ref, mod_ref, gain_ref, wg_ref, wu_ref, wo_ref, gain2_ref, o_ref, h_scr, acc_scr,
                *, mod_row, n_f, final):
    f = pl.program_id(1)

    @pl.when(f == 0)
    def _():
        y = _rms(x_ref[...], gain_ref[...])
        h = y * (1.0 + mod_ref[mod_row + 1:mod_row + 2, :]) + mod_ref[mod_row:mod_row + 1, :]
        h_scr[...] = h.astype(BF16)
        acc_scr[...] = jnp.zeros_like(acc_scr)

    h = h_scr[...]
    g = _dot(h, wg_ref[...])
    u = _dot(h, wu_ref[...])
    a = (g * _sigmoid(g) * u).astype(BF16)
    acc_scr[...] += _dot(a, wo_ref[...])

    @pl.when(f == n_f - 1)
    def _():
        xn = x_ref[...] + mod_ref[mod_row + 2:mod_row + 3, :] * (0.5 * acc_scr[...])
        if final:
            o_ref[...] = _rms(xn, gain2_ref[...])
        else:
            o_ref[...] = xn


def _ffn_call(x, modss, gain, w_in, w_out, gain2, *, mod_row, final, tm=1024, tf=512):
    n_f = D_FF // tf
    ss_per = SS_LEN // tm
    kern = functools.partial(_ffn_kernel, mod_row=mod_row, n_f=n_f, final=final)
    return pl.pallas_call(
        kern,
        out_shape=jax.ShapeDtypeStruct((N_TOK, D), F32),
        grid=(N_TOK // tm, n_f),
        in_specs=[
            pl.BlockSpec((tm, D), lambda i, f: (i, 0)),
            pl.BlockSpec((None, N_MOD, D), lambda i, f: (i // ss_per, 0, 0)),
            pl.BlockSpec((1, D), lambda i, f: (0, 0)),
            pl.BlockSpec((D, tf), lambda i, f: (0, f)),
            pl.BlockSpec((D, tf), lambda i, f: (0, n_f + f)),
            pl.BlockSpec((tf, D), lambda i, f: (f, 0)),
            pl.BlockSpec((1, D), lambda i, f: (0, 0)),
        ],
        out_specs=pl.BlockSpec((tm, D), lambda i, f: (i, 0)),
        scratch_shapes=[pltpu.VMEM((tm, D), BF16), pltpu.VMEM((tm, D), F32)],
        compiler_params=_cparams(("parallel", "arbitrary")),
        name="ffn_final" if final else "ffn",
    )(x, modss, gain, w_in, w_in, w_out, gain2)


def _mixin_rwkv_kernel(x_ref, mod_ref, gain_ref, w_ref, mu_ref, o_ref, h_scr):
    i = pl.program_id(0)
    j = pl.program_id(1)

    @pl.when(j == 0)
    def _():
        y = _rms(x_ref[...], gain_ref[...])
        h_scr[...] = (y * (1.0 + mod_ref[4:5, :]) + mod_ref[3:4, :]).astype(BF16)

    u = _dot(h_scr[...], w_ref[...])
    rows = u.shape[0]
    seq_len = jnp.where(i < N_CTX_SS, CTX_T, LAT_T)
    pos = lax.broadcasted_iota(jnp.int32, u.shape, 0) & (seq_len - 1)
    prev = jnp.where(pos == 0, 0.0, pltpu.roll(u, 1, 0))
    nxt = jnp.where(pos == seq_len - 1, 0.0, pltpu.roll(u, rows - 1, 0))
    o_ref[...] = u + mu_ref[...] * (0.5 * (prev + nxt) - u)


def _mixin_rwkv_call(x, modss, gain, w, mu, *, tn=384):
    tm = SS_LEN
    return pl.pallas_call(
        _mixin_rwkv_kernel,
        out_shape=jax.ShapeDtypeStruct((N_TOK, N_RWKV_COLS), F32),
        grid=(N_TOK // tm, N_RWKV_COLS // tn),
        in_specs=[
            pl.BlockSpec((tm, D), lambda i, j: (i, 0)),
            pl.BlockSpec((None, N_MOD, D), lambda i, j: (i, 0, 0)),
            pl.BlockSpec((1, D), lambda i, j: (0, 0)),
            pl.BlockSpec((D, tn), lambda i, j: (0, j)),
            pl.BlockSpec((1, tn), lambda i, j: (0, j)),
        ],
        out_specs=pl.BlockSpec((tm, tn), lambda i, j: (i, j)),
        scratch_shapes=[pltpu.VMEM((tm, D), BF16)],
        compiler_params=_cparams(("parallel", "arbitrary")),
        name="mixin_rwkv",
    )(x, modss, gain, w, mu)


def _mixin_gate_kernel(x_ref, mod_ref, gain_ref, wa_ref, wb_ref, wga_ref, wgb_ref, z_ref, ga_ref, gb_ref,
                       h_scr):
    j = pl.program_id(1)

    @pl.when(j == 0)
    def _():
        y = _rms(x_ref[...], gain_ref[...])
        h_scr[...] = (y * (1.0 + mod_ref[4:5, :]) + mod_ref[3:4, :]).astype(BF16)

    h = h_scr[...]
    z_ref[...] = _dot(h, wa_ref[...]) * _sigmoid(_dot(h, wb_ref[...]))
    ga_ref[...] = _sigmoid(_dot(h, wga_ref[...]))
    gb_ref[...] = _sigmoid(_dot(h, wgb_ref[...]))


def _mixin_gate_call(x, modss, gain, w_glu, w_gate, *, tm=1024, tn=256):
    n_j = D // tn
    ss_per = SS_LEN // tm
    sds = jax.ShapeDtypeStruct((N_TOK, D), F32)
    lo = lambda i, j: (0, j)
    hi = lambda i, j: (0, n_j + j)
    out = pl.BlockSpec((tm, tn), lambda i, j: (i, j))
    return pl.pallas_call(
        _mixin_gate_kernel,
        out_shape=(sds, sds, sds),
        grid=(N_TOK // tm, n_j),
        in_specs=[
            pl.BlockSpec((tm, D), lambda i, j: (i, 0)),
            pl.BlockSpec((None, N_MOD, D), lambda i, j: (i // ss_per, 0, 0)),
            pl.BlockSpec((1, D), lambda i, j: (0, 0)),
            pl.BlockSpec((D, tn), lo),
            pl.BlockSpec((D, tn), hi),
            pl.BlockSpec((D, tn), lo),
            pl.BlockSpec((D, tn), hi),
        ],
        out_specs=(out, out, out),
        scratch_shapes=[pltpu.VMEM((tm, D), BF16)],
        compiler_params=_cparams(("parallel", "arbitrary")),
        name="mixin_gate",
    )(x, modss, gain, w_glu, w_glu, w_gate, w_gate)


def _scan_prep(x, d, reverse, prm):
    (w0_ref, wup_ref, a0_ref, aup_ref, kk_ref, ka_ref, rk_ref, bd_ref) = prm
    c = CHUNK
    r = x[:, 0:D]
    k = x[:, D:2 * D]
    v = x[:, 2 * D:3 * D]
    wd = x[:, 3 * D + d * LORA_W:3 * D + (d + 1) * LORA_W]
    ad_off = 3 * D + 2 * LORA_W
    ad = x[:, ad_off + d * LORA_A:ad_off + (d + 1) * LORA_A]
    bd = bd_ref[...]

    logw = -DECAY_SCALE * _sigmoid(w0_ref[d] + _dot(jnp.tanh(wd).astype(BF16), wup_ref[d]))
    a = _sigmoid(a0_ref[d] + _dot(ad.astype(BF16), aup_ref[d]))
    kk = k * kk_ref[...]
    kk = kk * lax.rsqrt(_head_stat(kk * kk, bd) + 1e-12)
    kd = k * (1.0 + (a - 1.0) * ka_ref[...])
    b = kk * a
    bonus = _head_stat(r * kd * rk_ref[...], bd) * v

    ti = lax.broadcasted_iota(jnp.int32, (c, c), 0)
    si = lax.broadcasted_iota(jnp.int32, (c, c), 1)
    tri = jnp.where((ti <= si) if reverse else (ti >= si), 1.0, 0.0).astype(BF16)
    lw_hi, lw_lo = _split_bf16(logw)
    cum = _dot(tri, lw_hi) + _dot(tri, lw_lo)
    tot = cum[0:1, :] if reverse else cum[c - 1:c, :]
    p_inv = jnp.exp(-cum)
    p_tot = jnp.exp(tot)
    r_b = b * p_inv
    r_k = kd * p_inv
    return dict(
        l_kk=(kk * jnp.exp(cum - logw)).astype(BF16), l_r=(r * jnp.exp(cum)).astype(BF16),
        r_b=r_b, r_k=r_k, s_b=(r_b * p_tot).astype(BF16), s_k=(r_k * p_tot).astype(BF16),
        v=v, p_tot=p_tot, bonus=bonus)


def _scan_kernel(xf_ref, xb_ref, s0f_ref, s0b_ref, w0_ref, wup_ref, a0_ref, aup_ref, kk_ref, ka_ref,
                 rk_ref, bd_ref, yf_ref, yb_ref, bf_ref, bb_ref, sf_ref, sb_ref, s_scr):
    s = pl.program_id(0)
    is_ctx = s < CTX_STEPS
    n_chunks = jnp.where(is_ctx, CTX_CHUNKS, LAT_CHUNKS)
    cidx = jnp.where(is_ctx, s, s - CTX_STEPS) & (n_chunks - 1)
    prm = (w0_ref, wup_ref, a0_ref, aup_ref, kk_ref, ka_ref, rk_ref, bd_ref)
    c, g, gw = CHUNK, HEAD_GROUP, HEAD_GROUP * HEAD_DIM
    n_groups = N_HEADS // g
    s0_refs = (s0f_ref, s0b_ref)

    @pl.when(jnp.logical_and(cidx == 0, is_ctx))
    def _():
        s_scr[...] = jnp.zeros_like(s_scr)

    @pl.when(jnp.logical_and(cidx == 0, jnp.logical_not(is_ctx)))
    def _():
        for d in range(2):
            for gi in range(n_groups):
                for l in range(g):
                    pieces = [jnp.zeros((HEAD_DIM, HEAD_DIM), F32)] * g
                    pieces[l] = s0_refs[d][gi * g + l]
                    s_scr[d, gi, l * HEAD_DIM:(l + 1) * HEAD_DIM, :] = jnp.concatenate(pieces, axis=1)

    pre = (_scan_prep(xf_ref[...], 0, False, prm), _scan_prep(xb_ref[...], 1, True, prm))

    row = lax.broadcasted_iota(jnp.int32, (c, gw), 0)
    lane = lax.broadcasted_iota(jnp.int32, (c, gw), 1)
    col = lane & (HEAD_DIM - 1)
    head_of_lane = lane >> 6
    eye = jnp.where(row == col, 1.0, 0.0)
    strict = (row > col, row < col)
    incl = (row >= col, row <= col)
    srow = lax.broadcasted_iota(jnp.int32, (gw, gw), 0) >> 6
    slane = lax.broadcasted_iota(jnp.int32, (gw, gw), 1) >> 6
    state_mask = srow == slane
    nt = (((1,), (1,)), ((), ()))
    tn = (((0,), (0,)), ((), ()))

    def blockdiag(x):
        return jnp.concatenate([jnp.where(head_of_lane == l, x, 0.0) for l in range(g)], axis=0).astype(BF16)

    def same_block(n):
        shift = n.bit_length() - 1
        return (row >> shift) == (col >> shift)

    items = [(d, gi) for d in range(2) for gi in range(n_groups)]
    cols = lambda gi: slice(gi * gw, (gi + 1) * gw)

    s_old = [s_scr[d, gi] for d, gi in items]
    lhs = [jnp.concatenate([pre[d]["l_kk"][:, cols(gi)], pre[d]["l_r"][:, cols(gi)]], axis=0) for d, gi in items]
    rt = [jnp.concatenate([blockdiag(pre[d]["r_b"][:, cols(gi)]), blockdiag(pre[d]["r_k"][:, cols(gi)])], axis=0)
          for d, gi in items]
    amat = [lax.dot_general(lhs[i], rt[i], nt, preferred_element_type=F32) for i in range(len(items))]
    ls = [lax.dot_general(lhs[i], s_old[i].astype(BF16), nt, preferred_element_type=F32) for i in range(len(items))]
    a_ab = [jnp.where(strict[d], amat[i][:c, :gw], 0.0) for i, (d, gi) in enumerate(items)]
    a_ak = [jnp.where(strict[d], amat[i][:c, gw:], 0.0).astype(BF16) for i, (d, gi) in enumerate(items)]
    a_r = [jnp.concatenate([jnp.where(incl[d], amat[i][c:, :gw], 0.0), jnp.where(incl[d], amat[i][c:, gw:], 0.0)],
                           axis=1).astype(BF16) for i, (d, gi) in enumerate(items)]
    bd_v = [blockdiag(pre[d]["v"][:, cols(gi)]) for d, gi in items]
    rhs = [-(ls[i][:c] + _dot(a_ak[i], bd_v[i])) for i in range(len(items))]

    tinv = [eye - jnp.where(same_block(2), a, 0.0) for a in a_ab]
    n = 2
    while n < c:
        lower = jnp.logical_and(same_block(2 * n), jnp.logical_not(same_block(n)))
        off = [jnp.where(lower, a, 0.0).astype(BF16) for a in a_ab]
        w = [_dot(off[i], blockdiag(tinv[i])) for i in range(len(items))]
        tinv = [tinv[i] - _dot(tinv[i].astype(BF16), blockdiag(w[i])) for i in range(len(items))]
        n *= 2

    u = [_dot(tinv[i].astype(BF16), blockdiag(rhs[i])) for i in range(len(items))]
    y = [ls[i][c:] + _dot(a_r[i], jnp.concatenate([blockdiag(u[i]), bd_v[i]], axis=0)) for i in range(len(items))]
    for i, (d, gi) in enumerate(items):
        uv = jnp.concatenate([u[i].astype(BF16), pre[d]["v"][:, cols(gi)].astype(BF16)], axis=0)
        sbk = jnp.concatenate([pre[d]["s_b"][:, cols(gi)], pre[d]["s_k"][:, cols(gi)]], axis=0)
        upd = lax.dot_general(uv, sbk, tn, preferred_element_type=F32)
        s_scr[d, gi] = s_old[i] * pre[d]["p_tot"][:, cols(gi)] + jnp.where(state_mask, upd, 0.0)

    yf_ref[...] = jnp.concatenate(y[:n_groups], axis=1)
    yb_ref[...] = jnp.concatenate(y[n_groups:], axis=1)
    bf_ref[...] = pre[0]["bonus"]
    bb_ref[...] = pre[1]["bonus"]

    @pl.when(jnp.logical_and(cidx == n_chunks - 1, is_ctx))
    def _():
        for d, out_ref in enumerate((sf_ref, sb_ref)):
            for h in range(N_HEADS):
                gi, l = divmod(h, g)
                hs = slice(l * HEAD_DIM, (l + 1) * HEAD_DIM)
                out_ref[h] = s_scr[d, gi, hs, hs]


def _scan_row_block(s, reverse):
    ctx = s < CTX_STEPS
    n_chunks = jnp.where(ctx, CTX_CHUNKS, LAT_CHUNKS)
    t = jnp.where(ctx, s, s - CTX_STEPS)
    cidx = t & (n_chunks - 1)
    if reverse:
        cidx = n_chunks - 1 - cidx
    base = jnp.where(ctx, 0, CTX_STEPS)
    return base + (t - (t & (n_chunks - 1))) + cidx


def _scan_call(u, s0f, s0b, w0, wup, a0, aup, k_k, k_a, r_k, bd):
    fwd = lambda s: (_scan_row_block(s, False), 0)
    bwd = lambda s: (_scan_row_block(s, True), 0)
    lat_seq = lambda s: (jnp.maximum(s - CTX_STEPS, 0) // LAT_CHUNKS, 0, 0, 0)
    ctx_seq = lambda s: (jnp.minimum(s // CTX_CHUNKS, CTX_B - 1), 0, 0, 0)
    const2 = lambda s: (0, 0)
    const3 = lambda s: (0, 0, 0)
    st_block = (None, N_HEADS, HEAD_DIM, HEAD_DIM)
    y_sds = jax.ShapeDtypeStruct((N_TOK, D), F32)
    s_sds = jax.ShapeDtypeStruct((CTX_B, N_HEADS, HEAD_DIM, HEAD_DIM), F32)
    return pl.pallas_call(
        _scan_kernel,
        out_shape=(y_sds, y_sds, y_sds, y_sds, s_sds, s_sds),
        grid=(CTX_STEPS + LAT_STEPS,),
        in_specs=[
            pl.BlockSpec((CHUNK, N_RWKV_COLS), fwd),
            pl.BlockSpec((CHUNK, N_RWKV_COLS), bwd),
            pl.BlockSpec(st_block, lat_seq),
            pl.BlockSpec(st_block, lat_seq),
            pl.BlockSpec((2, 1, D), const3),
            pl.BlockSpec((2, LORA_W, D), const3),
            pl.BlockSpec((2, 1, D), const3),
            pl.BlockSpec((2, LORA_A, D), const3),
            pl.BlockSpec((1, D), const2),
            pl.BlockSpec((1, D), const2),
            pl.BlockSpec((1, D), const2),
            pl.BlockSpec((STAT_W, STAT_W), const2),
        ],
        out_specs=(
            pl.BlockSpec((CHUNK, D), fwd),
            pl.BlockSpec((CHUNK, D), bwd),
            pl.BlockSpec((CHUNK, D), fwd),
            pl.BlockSpec((CHUNK, D), bwd),
            pl.BlockSpec(st_block, ctx_seq),
            pl.BlockSpec(st_block, ctx_seq),
        ),
        scratch_shapes=[pltpu.VMEM((2, N_HEADS // HEAD_GROUP, HEAD_GROUP * HEAD_DIM, HEAD_GROUP * HEAD_DIM), F32)],
        compiler_params=_cparams(("arbitrary",)),
        name="wkv7_scan",
    )(u, u, s0f, s0b, w0, wup, a0, aup, k_k, k_a, r_k, bd)


def _conv_kernel(z_ref, zp_ref, zn_ref, w_ref, b_ref, lnw_ref, lnb_ref, o_ref, pad_scr, *, tm, halo):
    i = pl.program_id(0)
    tiles_per_lat = LAT_T // tm
    j = (i - N_CTX // tm) % tiles_per_lat
    is_ctx = i < N_CTX // tm
    keep_prev = jnp.where(jnp.logical_or(is_ctx, j == 0), 0.0, 1.0)
    keep_next = jnp.where(jnp.logical_or(is_ctx, j == tiles_per_lat - 1), 0.0, 1.0)
    pad_scr[0:halo, :] = zp_ref[...] * keep_prev
    pad_scr[halo:halo + tm, :] = z_ref[...]
    pad_scr[halo + tm:halo + tm + halo, :] = zn_ref[...] * keep_next

    rb = 64
    lanes = 128
    cols = []
    for cb in range(D // lanes):
        cs = slice(cb * lanes, (cb + 1) * lanes)
        blocks = []
        for r0 in range(0, tm, rb):
            acc = jnp.zeros((rb, lanes), F32) + b_ref[:, cs]
            for tap in range(CONV_WIDTH):
                start = r0 + tap + halo - CONV_HALF
                acc = acc + w_ref[tap:tap + 1, cs] * pad_scr[start:start + rb, cs]
            blocks.append(acc)
        cols.append(jnp.concatenate(blocks, axis=0))
    y = jnp.concatenate(cols, axis=1)
    mean = jnp.mean(y, axis=-1, keepdims=True)
    yc = y - mean
    var = jnp.mean(yc * yc, axis=-1, keepdims=True)
    y = yc * lax.rsqrt(var + LN_EPS) * lnw_ref[...] + lnb_ref[...]
    o_ref[...] = y * _sigmoid(y)


def _conv_call(z, w, b, lnw, lnb, *, tm=256, halo=16):
    hb = tm // halo
    n_h = N_TOK // halo
    kern = functools.partial(_conv_kernel, tm=tm, halo=halo)
    return pl.pallas_call(
        kern,
        out_shape=jax.ShapeDtypeStruct((N_TOK, D), F32),
        grid=(N_TOK // tm,),
        in_specs=[
            pl.BlockSpec((tm, D), lambda i: (i, 0)),
            pl.BlockSpec((halo, D), lambda i: (jnp.maximum(i * hb - 1, 0), 0)),
            pl.BlockSpec((halo, D), lambda i: (jnp.minimum((i + 1) * hb, n_h - 1), 0)),
            pl.BlockSpec((CONV_WIDTH, D), lambda i: (0, 0)),
            pl.BlockSpec((1, D), lambda i: (0, 0)),
            pl.BlockSpec((1, D), lambda i: (0, 0)),
            pl.BlockSpec((1, D), lambda i: (0, 0)),
        ],
        out_specs=pl.BlockSpec((tm, D), lambda i: (i, 0)),
        scratch_shapes=[pltpu.VMEM((tm + 2 * halo, D), F32)],
        compiler_params=_cparams(("parallel",)),
        name="conv_branch",
    )(z, z, z, w, b, lnw, lnb)


def _mixout_kernel(x_ref, mod_ref, yf_ref, yb_ref, bf_ref, bb_ref, gd_ref, ga_ref, gb_ref, yc_ref,
                   gup_ref, lnw_ref, lnb_ref, bd_ref, wo_ref, o_ref):
    bd = bd_ref[...]
    y = yf_ref[...] + yb_ref[...]
    yc = y - _head_stat(y, bd) * (1.0 / HEAD_DIM)
    var = _head_stat(yc * yc, bd) * (1.0 / HEAD_DIM)
    y = yc * lax.rsqrt(var + GN_EPS) * lnw_ref[...] + lnb_ref[...]
    y = y + bf_ref[...] + bb_ref[...]
    g = _dot(_sigmoid(gd_ref[...]).astype(BF16), gup_ref[...])
    mixed = ga_ref[...] * (y * g) + gb_ref[...] * yc_ref[...]
    o_ref[...] = x_ref[...] + mod_ref[5:6, :] * _dot(mixed.astype(BF16), wo_ref[...])


def _mixout_call(x, modss, yf, yb, bf, bb, u, ga, gb, yconv, g_up, lnw, lnb, bd, w_out, *, tm=256):
    ss_per = SS_LEN // tm
    tok = lambda i: (i, 0)
    const2 = lambda i: (0, 0)
    gd_blk = (3 * D + 2 * LORA_W + 2 * LORA_A) // LORA_G
    return pl.pallas_call(
        _mixout_kernel,
        out_shape=jax.ShapeDtypeStruct((N_TOK, D), F32),
        grid=(N_TOK // tm,),
        in_specs=[
            pl.BlockSpec((tm, D), tok),
            pl.BlockSpec((None, N_MOD, D), lambda i: (i // ss_per, 0, 0)),
            pl.BlockSpec((tm, D), tok),
            pl.BlockSpec((tm, D), tok),
            pl.BlockSpec((tm, D), tok),
            pl.BlockSpec((tm, D), tok),
            pl.BlockSpec((tm, LORA_G), lambda i: (i, gd_blk)),
            pl.BlockSpec((tm, D), tok),
            pl.BlockSpec((tm, D), tok),
            pl.BlockSpec((tm, D), tok),
            pl.BlockSpec((LORA_G, D), const2),
            pl.BlockSpec((1, D), const2),
            pl.BlockSpec((1, D), const2),
            pl.BlockSpec((STAT_W, STAT_W), const2),
            pl.BlockSpec((D, D), const2),
        ],
        out_specs=pl.BlockSpec((tm, D), tok),
        compiler_params=_cparams(("parallel",)),
        name="mixout",
    )(x, modss, yf, yb, bf, bb, u, ga, gb, yconv, g_up, lnw, lnb, bd, w_out)


def _grid_pos_embed(rows):
    row, col = jnp.meshgrid(jnp.arange(rows, dtype=F32), jnp.arange(GRID_W, dtype=F32), indexing="ij")
    n_freq = D // 4
    freqs = jnp.exp(-math.log(POS_MAX_PERIOD) * jnp.arange(n_freq, dtype=F32) / n_freq)
    ang_r = row.reshape(-1, 1) * freqs
    ang_c = col.reshape(-1, 1) * freqs
    return jnp.concatenate([jnp.sin(ang_r), jnp.cos(ang_r), jnp.sin(ang_c), jnp.cos(ang_c)], axis=-1)


def kernel(x_prompt, x_sample, state_rwkv_fwd, state_rwkv_bwd, c, c_ctx, w_mod, b_mod, norm_ffn1, ffn1_w_in, ffn1_w_out, norm_mix, mix_w_in, rwkv_mu, rwkv_w0, rwkv_w_up, rwkv_a0, rwkv_a_up, rwkv_g_up, rwkv_k_k, rwkv_k_a, rwkv_r_k, rwkv_lnx_w, rwkv_lnx_b, conv_dw, conv_db, conv_ln_w, conv_ln_b, mix_w_out, norm_ffn2, ffn2_w_in, ffn2_w_out, norm_final):
    assert x_prompt.shape == (CTX_B, CTX_T, D) and x_sample.shape == (LAT_B, LAT_T, D)
    assert w_mod.shape[0] == 1, "single-layer trunk"
    row = lambda t: t.reshape(1, -1)

    cond = jnp.concatenate([c, c_ctx[None], jnp.zeros((8 - LAT_B - 1, D), F32)], axis=0)
    mod = _mod_call(cond, w_mod[0], row(b_mod[0])).reshape(8, N_MOD, D)
    ss_cond = jnp.array([LAT_B] * N_CTX_SS + list(range(LAT_B)), jnp.int32)
    modss = mod[ss_cond]

    pos = _grid_pos_embed(LAT_T // GRID_W).astype(x_sample.dtype)
    x0 = jnp.concatenate([x_prompt.reshape(N_CTX, D), (x_sample + pos[None]).reshape(LAT_B * LAT_T, D)], axis=0)

    w_mix = mix_w_in[0]
    glu_off = N_RWKV_COLS
    gate_off = N_RWKV_COLS + 2 * D
    w_rwkv = w_mix[:, :glu_off].astype(BF16)
    w_glu = w_mix[:, glu_off:gate_off].astype(BF16)
    w_gate = w_mix[:, gate_off:].astype(BF16)

    x1 = _ffn_call(x0, modss, row(norm_ffn1[0]), ffn1_w_in[0].astype(BF16), ffn1_w_out[0].astype(BF16),
                   row(norm_final), mod_row=0, final=False)

    u = _mixin_rwkv_call(x1, modss, row(norm_mix[0]), w_rwkv, row(rwkv_mu[0]))
    z, ga, gb = _mixin_gate_call(x1, modss, row(norm_mix[0]), w_glu, w_gate)

    ii = lax.broadcasted_iota(jnp.int32, (STAT_W, STAT_W), 0) // HEAD_DIM
    jj = lax.broadcasted_iota(jnp.int32, (STAT_W, STAT_W), 1) // HEAD_DIM
    bd = jnp.where(ii == jj, 1.0, 0.0).astype(BF16)

    yf, yb, bonf, bonb, s_f, s_b = _scan_call(
        u, state_rwkv_fwd[:, 0], state_rwkv_bwd[:, 0],
        rwkv_w0[0].reshape(2, 1, D), rwkv_w_up[0].astype(BF16), rwkv_a0[0].reshape(2, 1, D),
        rwkv_a_up[0].astype(BF16), row(rwkv_k_k[0]), row(rwkv_k_a[0]), row(rwkv_r_k[0]), bd)

    yconv = _conv_call(z, conv_dw[0], row(conv_db[0]), row(conv_ln_w[0]), row(conv_ln_b[0]))

    x2 = _mixout_call(x1, modss, yf, yb, bonf, bonb, u, ga, gb, yconv, rwkv_g_up[0].astype(BF16),
                      row(rwkv_lnx_w[0]), row(rwkv_lnx_b[0]), bd, mix_w_out[0].astype(BF16))

    y = _ffn_call(x2, modss, row(norm_ffn2[0]), ffn2_w_in[0].astype(BF16), ffn2_w_out[0].astype(BF16),
                  row(norm_final), mod_row=6, final=True)

    y_prompt = y[:N_CTX].reshape(CTX_B, CTX_T, D)
    y_sample = y[N_CTX:].reshape(LAT_B, LAT_T, D)
    st = (CTX_B, 1, N_HEADS, HEAD_DIM, HEAD_DIM)
    return (y_prompt, y_sample, s_f.reshape(st).astype(x_prompt.dtype), s_b.reshape(st).astype(x_prompt.dtype))
```

```python
import functools
import math

import jax
import jax.numpy as jnp
from jax import lax
from jax.experimental import pallas as pl
from jax.experimental.pallas import tpu as pltpu

F32 = jnp.float32
BF16 = jnp.bfloat16

D = 1024
N_HEADS = 16
HEAD_DIM = 64
D_FF = 4 * D
N_MOD = 9
CONV_WIDTH = 31
CONV_HALF = CONV_WIDTH // 2
LORA_W = 64
LORA_A = 64
LORA_G = 128
N_RWKV_COLS = 3 * D + 2 * LORA_W + 2 * LORA_A + LORA_G
GRID_W = 64
EPS = 1e-6
LN_EPS = 1e-5
GN_EPS = 64e-5
DECAY_SCALE = 0.606531
POS_MAX_PERIOD = 10000.0

CTX_B, CTX_T = 32, 256
LAT_B, LAT_T = 4, 2048
N_CTX = CTX_B * CTX_T
N_TOK = N_CTX + LAT_B * LAT_T
SS_LEN = 2048
N_SS = N_TOK // SS_LEN
N_CTX_SS = N_CTX // SS_LEN

CHUNK = 64
HEAD_GROUP = 2
CTX_CHUNKS = CTX_T // CHUNK
LAT_CHUNKS = LAT_T // CHUNK
CTX_STEPS = CTX_B * CTX_CHUNKS
LAT_STEPS = LAT_B * LAT_CHUNKS
STAT_W = 256
SUBLANES = 8

VMEM_LIMIT = 56 * 1024 * 1024


def _cparams(sem):
    return pltpu.CompilerParams(dimension_semantics=sem, vmem_limit_bytes=VMEM_LIMIT)


def _sigmoid(x):
    return 1.0 / (1.0 + jnp.exp(-x))


def _dot(a, b):
    return jnp.dot(a, b, preferred_element_type=F32)


def _split_bf16(x):
    hi = x.astype(BF16)
    lo = (x - hi.astype(F32)).astype(BF16)
    return hi, lo


def _head_stat(x, bd):
    groups = D // STAT_W
    stacked = jnp.concatenate([x[:, g * STAT_W:(g + 1) * STAT_W] for g in range(groups)], axis=0)
    s = _dot(stacked.astype(BF16), bd)
    rows = x.shape[0]
    return jnp.concatenate([s[g * rows:(g + 1) * rows] for g in range(groups)], axis=1)


def _mod_kernel(cond_ref, w_ref, b_ref, o_ref):
    c = cond_ref[...]
    s = (c * _sigmoid(c)).astype(BF16)
    o_ref[...] = _dot(s, w_ref[...].astype(BF16)) + b_ref[...]


def _mod_call(cond, w_mod, b_mod):
    tn = 1152
    n = N_MOD * D
    return pl.pallas_call(
        _mod_kernel,
        out_shape=jax.ShapeDtypeStruct((8, n), F32),
        grid=(n // tn,),
        in_specs=[
            pl.BlockSpec((8, D), lambda j: (0, 0)),
            pl.BlockSpec((D, tn), lambda j: (0, j)),
            pl.BlockSpec((1, tn), lambda j: (0, j)),
        ],
        out_specs=pl.BlockSpec((8, tn), lambda j: (0, j)),
        compiler_params=_cparams(("arbitrary",)),
        name="mod",
    )(cond, w_mod, b_mod)


def _rms(x, gain):
    return x * lax.rsqrt(jnp.mean(x * x, axis=-1, keepdims=True) + EPS) * gain


def _ffn_start(x, mod_ref, gain_ref, h_scr, acc_scr, mod_row):
    y = _rms(x, gain_ref[...])
    h = y * (1.0 + mod_ref[mod_row + 1:mod_row + 2, :]) + mod_ref[mod_row:mod_row + 1, :]
    h_scr[...] = h.astype(BF16)
    acc_scr[...] = jnp.zeros_like(acc_scr)


def _ffn_step(wg_ref, wu_ref, wo_ref, h_scr, acc_scr):
    h = h_scr[...]
    g = _dot(h, wg_ref[...])
    u = _dot(h, wu_ref[...])
    a = (g * _sigmoid(g) * u).astype(BF16)
    acc_scr[...] += _dot(a, wo_ref[...])


def _ffn_first_kernel(xc_ref, xl_ref, pos_ref, mod_ref, gain_ref, wg_ref, wu_ref, wo_ref, o_ref,
                      x_scr, h_scr, acc_scr, *, n_f, ctx_tiles):
    i = pl.program_id(0)
    f = pl.program_id(1)

    @pl.when(jnp.logical_and(f == 0, i < ctx_tiles))
    def _():
        x_scr[...] = xc_ref[...]

    @pl.when(jnp.logical_and(f == 0, i >= ctx_tiles))
    def _():
        x_scr[...] = xl_ref[...] + pos_ref[...]

    @pl.when(f == 0)
    def _():
        _ffn_start(x_scr[...], mod_ref, gain_ref, h_scr, acc_scr, 0)

    _ffn_step(wg_ref, wu_ref, wo_ref, h_scr, acc_scr)

    @pl.when(f == n_f - 1)
    def _():
        o_ref[...] = x_scr[...] + mod_ref[2:3, :] * (0.5 * acc_scr[...])


def _ffn_last_kernel(x_ref, mod_ref, gain_ref, wg_ref, wu_ref, wo_ref, gain2_ref, oc_ref, ol_ref,
                     h_scr, acc_scr, *, n_f, ctx_tiles):
    i = pl.program_id(0)
    f = pl.program_id(1)

    @pl.when(f == 0)
    def _():
        _ffn_start(x_ref[...], mod_ref, gain_ref, h_scr, acc_scr, 6)

    _ffn_step(wg_ref, wu_ref, wo_ref, h_scr, acc_scr)

    def result():
        return _rms(x_ref[...] + mod_ref[8:9, :] * (0.5 * acc_scr[...]), gain2_ref[...])

    @pl.when(jnp.logical_and(f == n_f - 1, i < ctx_tiles))
    def _():
        oc_ref[...] = result()

    @pl.when(jnp.logical_and(f == n_f - 1, i >= ctx_tiles))
    def _():
        ol_ref[...] = result()


def _ffn_specs(tm, tf):
    n_f = D_FF // tf
    ss_per = SS_LEN // tm
    return [
        pl.BlockSpec((None, N_MOD, D), lambda i, f: (i // ss_per, 0, 0)),
        pl.BlockSpec((1, D), lambda i, f: (0, 0)),
        pl.BlockSpec((D, tf), lambda i, f: (0, f)),
        pl.BlockSpec((D, tf), lambda i, f: (0, n_f + f)),
        pl.BlockSpec((tf, D), lambda i, f: (f, 0)),
    ]


def _ffn_first_call(xc, xl, pos, modss, gain, w_in, w_out, *, tm=1024, tf=512):
    n_f = D_FF // tf
    ctx_tiles = N_CTX // tm
    pos_tiles = LAT_T // tm
    once = pl.Buffered(1)
    kern = functools.partial(_ffn_first_kernel, n_f=n_f, ctx_tiles=ctx_tiles)
    return pl.pallas_call(
        kern,
        out_shape=jax.ShapeDtypeStruct((N_TOK, D), F32),
        grid=(N_TOK // tm, n_f),
        in_specs=[
            pl.BlockSpec((tm, D), lambda i, f: (jnp.minimum(i, ctx_tiles - 1), 0), pipeline_mode=once),
            pl.BlockSpec((tm, D), lambda i, f: (jnp.maximum(i - ctx_tiles, 0), 0), pipeline_mode=once),
            pl.BlockSpec((tm, D), lambda i, f: (jnp.maximum(i - ctx_tiles, 0) % pos_tiles, 0), pipeline_mode=once),
        ] + _ffn_specs(tm, tf),
        out_specs=pl.BlockSpec((tm, D), lambda i, f: (i, 0)),
        scratch_shapes=[pltpu.VMEM((tm, D), F32), pltpu.VMEM((tm, D), BF16), pltpu.VMEM((tm, D), F32)],
        compiler_params=_cparams(("parallel", "arbitrary")),
        name="ffn_first",
    )(xc, xl, pos, modss, gain, w_in, w_in, w_out)


def _ffn_last_call(x, modss, gain, w_in, w_out, gain2, *, tm=1024, tf=512):
    n_f = D_FF // tf
    ctx_tiles = N_CTX // tm
    kern = functools.partial(_ffn_last_kernel, n_f=n_f, ctx_tiles=ctx_tiles)
    return pl.pallas_call(
        kern,
        out_shape=(jax.ShapeDtypeStruct((N_CTX, D), F32), jax.ShapeDtypeStruct((N_TOK - N_CTX, D), F32)),
        grid=(N_TOK // tm, n_f),
        in_specs=[pl.BlockSpec((tm, D), lambda i, f: (i, 0))] + _ffn_specs(tm, tf) + [
            pl.BlockSpec((1, D), lambda i, f: (0, 0))],
        out_specs=(pl.BlockSpec((tm, D), lambda i, f: (jnp.minimum(i, ctx_tiles - 1), 0)),
                   pl.BlockSpec((tm, D), lambda i, f: (jnp.maximum(i - ctx_tiles, 0), 0))),
        scratch_shapes=[pltpu.VMEM((tm, D), BF16), pltpu.VMEM((tm, D), F32)],
        compiler_params=_cparams(("parallel", "arbitrary")),
        name="ffn_last",
    )(x, modss, gain, w_in, w_in, w_out, gain2)


def _mixin_rwkv_kernel(x_ref, mod_ref, gain_ref, w_ref, mu_ref, o_ref, h_scr):
    i = pl.program_id(0)
    j = pl.program_id(1)

    @pl.when(j == 0)
    def _():
        y = _rms(x_ref[...], gain_ref[...])
        h_scr[...] = (y * (1.0 + mod_ref[4:5, :]) + mod_ref[3:4, :]).astype(BF16)

    u = _dot(h_scr[...], w_ref[...])
    rows = u.shape[0]
    seq_len = jnp.where(i < N_CTX_SS, CTX_T, LAT_T)
    pos = lax.broadcasted_iota(jnp.int32, u.shape, 0) & (seq_len - 1)
    prev = jnp.where(pos == 0, 0.0, pltpu.roll(u, 1, 0))
    nxt = jnp.where(pos == seq_len - 1, 0.0, pltpu.roll(u, rows - 1, 0))
    o_ref[...] = u + mu_ref[...] * (0.5 * (prev + nxt) - u)


def _mixin_rwkv_call(x, modss, gain, w, mu, *, tn=384):
    tm = SS_LEN
    return pl.pallas_call(
        _mixin_rwkv_kernel,
        out_shape=jax.ShapeDtypeStruct((N_TOK, N_RWKV_COLS), F32),
        grid=(N_TOK // tm, N_RWKV_COLS // tn),
        in_specs=[
            pl.BlockSpec((tm, D), lambda i, j: (i, 0)),
            pl.BlockSpec((None, N_MOD, D), lambda i, j: (i, 0, 0)),
            pl.BlockSpec((1, D), lambda i, j: (0, 0)),
            pl.BlockSpec((D, tn), lambda i, j: (0, j)),
            pl.BlockSpec((1, tn), lambda i, j: (0, j)),
        ],
        out_specs=pl.BlockSpec((tm, tn), lambda i, j: (i, j)),
        scratch_shapes=[pltpu.VMEM((tm, D), BF16)],
        compiler_params=_cparams(("parallel", "arbitrary")),
        name="mixin_rwkv",
    )(x, modss, gain, w, mu)


def _mixin_gate_kernel(x_ref, mod_ref, gain_ref, wa_ref, wb_ref, wga_ref, wgb_ref, z_ref, ga_ref, gb_ref,
                       h_scr):
    j = pl.program_id(1)

    @pl.when(j == 0)
    def _():
        y = _rms(x_ref[...], gain_ref[...])
        h_scr[...] = (y * (1.0 + mod_ref[4:5, :]) + mod_ref[3:4, :]).astype(BF16)

    h = h_scr[...]
    z_ref[...] = _dot(h, wa_ref[...]) * _sigmoid(_dot(h, wb_ref[...]))
    ga_ref[...] = _sigmoid(_dot(h, wga_ref[...]))
    gb_ref[...] = _sigmoid(_dot(h, wgb_ref[...]))


def _mixin_gate_call(x, modss, gain, w_glu, w_gate, *, tm=1024, tn=256):
    n_j = D // tn
    ss_per = SS_LEN // tm
    sds = jax.ShapeDtypeStruct((N_TOK, D), F32)
    lo = lambda i, j: (0, j)
    hi = lambda i, j: (0, n_j + j)
    out = pl.BlockSpec((tm, tn), lambda i, j: (i, j))
    return pl.pallas_call(
        _mixin_gate_kernel,
        out_shape=(sds, sds, sds),
        grid=(N_TOK // tm, n_j),
        in_specs=[
            pl.BlockSpec((tm, D), lambda i, j: (i, 0)),
            pl.BlockSpec((None, N_MOD, D), lambda i, j: (i // ss_per, 0, 0)),
            pl.BlockSpec((1, D), lambda i, j: (0, 0)),
            pl.BlockSpec((D, tn), lo),
            pl.BlockSpec((D, tn), hi),
            pl.BlockSpec((D, tn), lo),
            pl.BlockSpec((D, tn), hi),
        ],
        out_specs=(out, out, out),
        scratch_shapes=[pltpu.VMEM((tm, D), BF16)],
        compiler_params=_cparams(("parallel", "arbitrary")),
        name="mixin_gate",
    )(x, modss, gain, w_glu, w_glu, w_gate, w_gate)


def _scan_prep(x, d, reverse, prm):
    (w0_ref, wup_ref, a0_ref, aup_ref, kk_ref, ka_ref, rk_ref, bd_ref) = prm
    c = CHUNK
    r = x[:, 0:D]
    k = x[:, D:2 * D]
    v = x[:, 2 * D:3 * D]
    wd = x[:, 3 * D + d * LORA_W:3 * D + (d + 1) * LORA_W]
    ad_off = 3 * D + 2 * LORA_W
    ad = x[:, ad_off + d * LORA_A:ad_off + (d + 1) * LORA_A]
    bd = bd_ref[...]

    logw = -DECAY_SCALE * _sigmoid(w0_ref[d] + _dot(jnp.tanh(wd).astype(BF16), wup_ref[d]))
    a = _sigmoid(a0_ref[d] + _dot(ad.astype(BF16), aup_ref[d]))
    kk = k * kk_ref[...]
    kk = kk * lax.rsqrt(_head_stat(kk * kk, bd) + 1e-12)
    kd = k * (1.0 + (a - 1.0) * ka_ref[...])
    b = kk * a
    bonus = _head_stat(r * kd * rk_ref[...], bd) * v

    ti = lax.broadcasted_iota(jnp.int32, (c, c), 0)
    si = lax.broadcasted_iota(jnp.int32, (c, c), 1)
    tri = jnp.where((ti <= si) if reverse else (ti >= si), 1.0, 0.0).astype(BF16)
    lw_hi, lw_lo = _split_bf16(logw)
    cum = _dot(tri, lw_hi) + _dot(tri, lw_lo)
    tot = cum[0:1, :] if reverse else cum[c - 1:c, :]
    p_inv = jnp.exp(-cum)
    p_tot = jnp.exp(tot)
    r_b = b * p_inv
    r_k = kd * p_inv
    return dict(
        l_kk=(kk * jnp.exp(cum - logw)).astype(BF16), l_r=(r * jnp.exp(cum)).astype(BF16),
        r_b=r_b, r_k=r_k, s_b=(r_b * p_tot).astype(BF16), s_k=(r_k * p_tot).astype(BF16),
        v=v, p_tot=p_tot, bonus=bonus)


def _group_norm(y, bd, lnw, lnb):
    yc = y - _head_stat(y, bd) * (1.0 / HEAD_DIM)
    var = _head_stat(yc * yc, bd) * (1.0 / HEAD_DIM)
    return yc * lax.rsqrt(var + GN_EPS) * lnw + lnb


def _scan_kernel(xf_ref, xb_ref, s0f_ref, s0b_ref, w0_ref, wup_ref, a0_ref, aup_ref, kk_ref, ka_ref,
                 rk_ref, bd_ref, lnw_ref, lnb_ref, y_ref, sf_ref, sb_ref, s_scr, bonus_scr):
    s = pl.program_id(0)
    is_ctx = s < CTX_STEPS
    n_chunks = jnp.where(is_ctx, CTX_CHUNKS, LAT_CHUNKS)
    t_step = jnp.where(is_ctx, s, s - CTX_STEPS)
    cidx = t_step & (n_chunks - 1)
    seq_in_ss = jnp.where(is_ctx, (t_step >> (CTX_CHUNKS.bit_length() - 1)) & (SS_LEN // CTX_T - 1), 0)
    row_f = pl.multiple_of((seq_in_ss * n_chunks + cidx) * CHUNK, CHUNK)
    row_b = pl.multiple_of((seq_in_ss * n_chunks + n_chunks - 1 - cidx) * CHUNK, CHUNK)
    second_visit = cidx >= (n_chunks >> 1)
    prm = (w0_ref, wup_ref, a0_ref, aup_ref, kk_ref, ka_ref, rk_ref, bd_ref)
    c, g, gw = CHUNK, HEAD_GROUP, HEAD_GROUP * HEAD_DIM
    n_groups = N_HEADS // g
    s0_refs = (s0f_ref, s0b_ref)

    @pl.when(jnp.logical_and(cidx == 0, is_ctx))
    def _():
        s_scr[...] = jnp.zeros_like(s_scr)

    @pl.when(jnp.logical_and(cidx == 0, jnp.logical_not(is_ctx)))
    def _():
        for d in range(2):
            for gi in range(n_groups):
                for l in range(g):
                    pieces = [jnp.zeros((HEAD_DIM, HEAD_DIM), F32)] * g
                    pieces[l] = s0_refs[d][gi * g + l]
                    s_scr[d, gi, l * HEAD_DIM:(l + 1) * HEAD_DIM, :] = jnp.concatenate(pieces, axis=1)

    pre = (_scan_prep(xf_ref[...], 0, False, prm), _scan_prep(xb_ref[...], 1, True, prm))

    row = lax.broadcasted_iota(jnp.int32, (c, gw), 0)
    lane = lax.broadcasted_iota(jnp.int32, (c, gw), 1)
    col = lane & (HEAD_DIM - 1)
    head_of_lane = lane >> 6
    eye = jnp.where(row == col, 1.0, 0.0)
    strict = (row > col, row < col)
    incl = (row >= col, row <= col)
    srow = lax.broadcasted_iota(jnp.int32, (gw, gw), 0) >> 6
    slane = lax.broadcasted_iota(jnp.int32, (gw, gw), 1) >> 6
    state_mask = srow == slane
    nt = (((1,), (1,)), ((), ()))
    tn = (((0,), (0,)), ((), ()))

    def blockdiag(x):
        return jnp.concatenate([jnp.where(head_of_lane == l, x, 0.0) for l in range(g)], axis=0).astype(BF16)

    def same_block(n):
        shift = n.bit_length() - 1
        return (row >> shift) == (col >> shift)

    items = [(d, gi) for d in range(2) for gi in range(n_groups)]
    cols = lambda gi: slice(gi * gw, (gi + 1) * gw)

    s_old = [s_scr[d, gi] for d, gi in items]
    lhs = [jnp.concatenate([pre[d]["l_kk"][:, cols(gi)], pre[d]["l_r"][:, cols(gi)]], axis=0) for d, gi in items]
    rt = [jnp.concatenate([blockdiag(pre[d]["r_b"][:, cols(gi)]), blockdiag(pre[d]["r_k"][:, cols(gi)])], axis=0)
          for d, gi in items]
    amat = [lax.dot_general(lhs[i], rt[i], nt, preferred_element_type=F32) for i in range(len(items))]
    ls = [lax.dot_general(lhs[i], s_old[i].astype(BF16), nt, preferred_element_type=F32) for i in range(len(items))]
    a_ab = [jnp.where(strict[d], amat[i][:c, :gw], 0.0) for i, (d, gi) in enumerate(items)]
    a_ak = [jnp.where(strict[d], amat[i][:c, gw:], 0.0).astype(BF16) for i, (d, gi) in enumerate(items)]
    a_r = [jnp.concatenate([jnp.where(incl[d], amat[i][c:, :gw], 0.0), jnp.where(incl[d], amat[i][c:, gw:], 0.0)],
                           axis=1).astype(BF16) for i, (d, gi) in enumerate(items)]
    bd_v = [blockdiag(pre[d]["v"][:, cols(gi)]) for d, gi in items]
    rhs = [-(ls[i][:c] + _dot(a_ak[i], bd_v[i])) for i in range(len(items))]

    tinv = [eye - jnp.where(same_block(2), a, 0.0) for a in a_ab]
    n = 2
    while n < c:
        lower = jnp.logical_and(same_block(2 * n), jnp.logical_not(same_block(n)))
        off = [jnp.where(lower, a, 0.0).astype(BF16) for a in a_ab]
        w = [_dot(off[i], blockdiag(tinv[i])) for i in range(len(items))]
        tinv = [tinv[i] - _dot(tinv[i].astype(BF16), blockdiag(w[i])) for i in range(len(items))]
        n *= 2

    u = [_dot(tinv[i].astype(BF16), blockdiag(rhs[i])) for i in range(len(items))]
    y = [ls[i][c:] + _dot(a_r[i], jnp.concatenate([blockdiag(u[i]), bd_v[i]], axis=0)) for i in range(len(items))]
    for i, (d, gi) in enumerate(items):
        uv = jnp.concatenate([u[i].astype(BF16), pre[d]["v"][:, cols(gi)].astype(BF16)], axis=0)
        sbk = jnp.concatenate([pre[d]["s_b"][:, cols(gi)], pre[d]["s_k"][:, cols(gi)]], axis=0)
        upd = lax.dot_general(uv, sbk, tn, preferred_element_type=F32)
        s_scr[d, gi] = s_old[i] * pre[d]["p_tot"][:, cols(gi)] + jnp.where(state_mask, upd, 0.0)

    visits = ((row_f, jnp.concatenate(y[:n_groups], axis=1), pre[0]["bonus"]),
              (row_b, jnp.concatenate(y[n_groups:], axis=1), pre[1]["bonus"]))

    @pl.when(jnp.logical_not(second_visit))
    def _():
        for rows, y_d, bonus_d in visits:
            y_ref[pl.ds(rows, c), :] = y_d
            bonus_scr[pl.ds(rows, c), :] = bonus_d

    @pl.when(second_visit)
    def _():
        for rows, y_d, bonus_d in visits:
            y_sum = y_ref[pl.ds(rows, c), :] + y_d
            y_ref[pl.ds(rows, c), :] = (_group_norm(y_sum, bd_ref[...], lnw_ref[...], lnb_ref[...])
                                         + bonus_scr[pl.ds(rows, c), :] + bonus_d)

    @pl.when(jnp.logical_and(cidx == n_chunks - 1, is_ctx))
    def _():
        for d, out_ref in enumerate((sf_ref, sb_ref)):
            for h in range(N_HEADS):
                gi, l = divmod(h, g)
                hs = slice(l * HEAD_DIM, (l + 1) * HEAD_DIM)
                out_ref[h] = s_scr[d, gi, hs, hs]


def _scan_row_block(s, reverse):
    ctx = s < CTX_STEPS
    n_chunks = jnp.where(ctx, CTX_CHUNKS, LAT_CHUNKS)
    t = jnp.where(ctx, s, s - CTX_STEPS)
    cidx = t & (n_chunks - 1)
    if reverse:
        cidx = n_chunks - 1 - cidx
    base = jnp.where(ctx, 0, CTX_STEPS)
    return base + (t - (t & (n_chunks - 1))) + cidx


def _scan_call(u, s0f, s0b, w0, wup, a0, aup, k_k, k_a, r_k, bd, lnw, lnb):
    assert CTX_STEPS // N_CTX_SS == LAT_CHUNKS, "context and latent super sequences take the same number of steps"
    steps_per_ss = LAT_CHUNKS
    fwd = lambda s: (_scan_row_block(s, False), 0)
    bwd = lambda s: (_scan_row_block(s, True), 0)
    lat_seq = lambda s: (jnp.maximum(s - CTX_STEPS, 0) // LAT_CHUNKS, 0, 0, 0)
    ctx_seq = lambda s: (jnp.minimum(s // CTX_CHUNKS, CTX_B - 1), 0, 0, 0)
    const2 = lambda s: (0, 0)
    const3 = lambda s: (0, 0, 0)
    st_block = (None, N_HEADS, HEAD_DIM, HEAD_DIM)
    y_sds = jax.ShapeDtypeStruct((N_TOK, D), F32)
    s_sds = jax.ShapeDtypeStruct((CTX_B, N_HEADS, HEAD_DIM, HEAD_DIM), F32)
    return pl.pallas_call(
        _scan_kernel,
        out_shape=(y_sds, s_sds, s_sds),
        grid=(CTX_STEPS + LAT_STEPS,),
        in_specs=[
            pl.BlockSpec((CHUNK, N_RWKV_COLS), fwd),
            pl.BlockSpec((CHUNK, N_RWKV_COLS), bwd),
            pl.BlockSpec(st_block, lat_seq),
            pl.BlockSpec(st_block, lat_seq),
            pl.BlockSpec((2, 1, D), const3),
            pl.BlockSpec((2, LORA_W, D), const3),
            pl.BlockSpec((2, 1, D), const3),
            pl.BlockSpec((2, LORA_A, D), const3),
            pl.BlockSpec((1, D), const2),
            pl.BlockSpec((1, D), const2),
            pl.BlockSpec((1, D), const2),
            pl.BlockSpec((STAT_W, STAT_W), const2),
            pl.BlockSpec((1, D), const2),
            pl.BlockSpec((1, D), const2),
        ],
        out_specs=(
            pl.BlockSpec((SS_LEN, D), lambda s: (s // steps_per_ss, 0)),
            pl.BlockSpec(st_block, ctx_seq),
            pl.BlockSpec(st_block, ctx_seq),
        ),
        scratch_shapes=[
            pltpu.VMEM((2, N_HEADS // HEAD_GROUP, HEAD_GROUP * HEAD_DIM, HEAD_GROUP * HEAD_DIM), F32),
            pltpu.VMEM((SS_LEN, D), F32),
        ],
        compiler_params=_cparams(("arbitrary",)),
        name="wkv7_scan",
    )(u, u, s0f, s0b, w0, wup, a0, aup, k_k, k_a, r_k, bd, lnw, lnb)


def _conv_kernel(z_ref, zp_ref, zn_ref, w_ref, b_ref, lnw_ref, lnb_ref, o_ref, pad_scr, *, tm, halo):
    i = pl.program_id(0)
    tiles_per_lat = LAT_T // tm
    j = (i - N_CTX // tm) % tiles_per_lat
    is_ctx = i < N_CTX // tm
    keep_prev = jnp.where(jnp.logical_or(is_ctx, j == 0), 0.0, 1.0)
    keep_next = jnp.where(jnp.logical_or(is_ctx, j == tiles_per_lat - 1), 0.0, 1.0)
    pad_scr[0, 0:halo, :] = zp_ref[...] * keep_prev
    pad_scr[0, halo:halo + tm, :] = z_ref[...]
    pad_scr[0, halo + tm:halo + tm + halo, :] = zn_ref[...] * keep_next
    span = tm + 2 * halo - SUBLANES
    for m in range(1, SUBLANES):
        pad_scr[m, 0:span, :] = pad_scr[0, m:m + span, :]

    rb = 128
    lanes = 128

    def lane_block(cb, carry):
        cs = pl.ds(pl.multiple_of(cb * lanes, lanes), lanes)
        for r0 in range(0, tm, rb):
            acc = jnp.broadcast_to(b_ref[:, cs], (rb, lanes))
            for tap in range(CONV_WIDTH):
                q, m = divmod(tap + halo - CONV_HALF, SUBLANES)
                start = r0 + q * SUBLANES
                acc = acc + w_ref[tap:tap + 1, cs] * pad_scr[m, start:start + rb, cs]
            o_ref[r0:r0 + rb, cs] = acc
        return carry

    lax.fori_loop(0, D // lanes, lane_block, 0)
    y = o_ref[...]
    mean = jnp.mean(y, axis=-1, keepdims=True)
    yc = y - mean
    var = jnp.mean(yc * yc, axis=-1, keepdims=True)
    y = yc * lax.rsqrt(var + LN_EPS) * lnw_ref[...] + lnb_ref[...]
    o_ref[...] = y * _sigmoid(y)


def _conv_call(z, w, b, lnw, lnb, *, tm=256, halo=16):
    hb = tm // halo
    n_h = N_TOK // halo
    kern = functools.partial(_conv_kernel, tm=tm, halo=halo)
    return pl.pallas_call(
        kern,
        out_shape=jax.ShapeDtypeStruct((N_TOK, D), F32),
        grid=(N_TOK // tm,),
        in_specs=[
            pl.BlockSpec((tm, D), lambda i: (i, 0)),
            pl.BlockSpec((halo, D), lambda i: (jnp.maximum(i * hb - 1, 0), 0)),
            pl.BlockSpec((halo, D), lambda i: (jnp.minimum((i + 1) * hb, n_h - 1), 0)),
            pl.BlockSpec((CONV_WIDTH, D), lambda i: (0, 0)),
            pl.BlockSpec((1, D), lambda i: (0, 0)),
            pl.BlockSpec((1, D), lambda i: (0, 0)),
            pl.BlockSpec((1, D), lambda i: (0, 0)),
        ],
        out_specs=pl.BlockSpec((tm, D), lambda i: (i, 0)),
        scratch_shapes=[pltpu.VMEM((SUBLANES, tm + 2 * halo, D), F32)],
        compiler_params=_cparams(("parallel",)),
        name="conv_branch",
    )(z, z, z, w, b, lnw, lnb)


def _mixout_kernel(x_ref, mod_ref, y_ref, gd_ref, ga_ref, gb_ref, yc_ref, gup_ref, wo_ref, o_ref):
    g = _dot(_sigmoid(gd_ref[...]).astype(BF16), gup_ref[...])
    mixed = ga_ref[...] * (y_ref[...] * g) + gb_ref[...] * yc_ref[...]
    o_ref[...] = x_ref[...] + mod_ref[5:6, :] * _dot(mixed.astype(BF16), wo_ref[...])


def _mixout_call(x, modss, y_att, u, ga, gb, yconv, g_up, w_out, *, tm=512):
    ss_per = SS_LEN // tm
    tok = lambda i: (i, 0)
    const2 = lambda i: (0, 0)
    gd_blk = (3 * D + 2 * LORA_W + 2 * LORA_A) // LORA_G
    return pl.pallas_call(
        _mixout_kernel,
        out_shape=jax.ShapeDtypeStruct((N_TOK, D), F32),
        grid=(N_TOK // tm,),
        in_specs=[
            pl.BlockSpec((tm, D), tok),
            pl.BlockSpec((None, N_MOD, D), lambda i: (i // ss_per, 0, 0)),
            pl.BlockSpec((tm, D), tok),
            pl.BlockSpec((tm, LORA_G), lambda i: (i, gd_blk)),
            pl.BlockSpec((tm, D), tok),
            pl.BlockSpec((tm, D), tok),
            pl.BlockSpec((tm, D), tok),
            pl.BlockSpec((LORA_G, D), const2),
            pl.BlockSpec((D, D), const2),
        ],
        out_specs=pl.BlockSpec((tm, D), tok),
        compiler_params=_cparams(("parallel",)),
        name="mixout",
    )(x, modss, y_att, u, ga, gb, yconv, g_up, w_out)


def _grid_pos_embed(rows):
    row, col = jnp.meshgrid(jnp.arange(rows, dtype=F32), jnp.arange(GRID_W, dtype=F32), indexing="ij")
    n_freq = D // 4
    freqs = jnp.exp(-math.log(POS_MAX_PERIOD) * jnp.arange(n_freq, dtype=F32) / n_freq)
    ang_r = row.reshape(-1, 1) * freqs
    ang_c = col.reshape(-1, 1) * freqs
    return jnp.concatenate([jnp.sin(ang_r), jnp.cos(ang_r), jnp.sin(ang_c), jnp.cos(ang_c)], axis=-1)


def kernel(x_prompt, x_sample, state_rwkv_fwd, state_rwkv_bwd, c, c_ctx, w_mod, b_mod, norm_ffn1, ffn1_w_in, ffn1_w_out, norm_mix, mix_w_in, rwkv_mu, rwkv_w0, rwkv_w_up, rwkv_a0, rwkv_a_up, rwkv_g_up, rwkv_k_k, rwkv_k_a, rwkv_r_k, rwkv_lnx_w, rwkv_lnx_b, conv_dw, conv_db, conv_ln_w, conv_ln_b, mix_w_out, norm_ffn2, ffn2_w_in, ffn2_w_out, norm_final):
    assert x_prompt.shape == (CTX_B, CTX_T, D) and x_sample.shape == (LAT_B, LAT_T, D)
    assert w_mod.shape[0] == 1, "single-layer trunk"
    row = lambda t: t.reshape(1, -1)

    cond = jnp.concatenate([c, c_ctx[None], jnp.zeros((8 - LAT_B - 1, D), F32)], axis=0)
    mod = _mod_call(cond, w_mod[0], row(b_mod[0])).reshape(8, N_MOD, D)
    ss_cond = jnp.array([LAT_B] * N_CTX_SS + list(range(LAT_B)), jnp.int32)
    modss = mod[ss_cond]

    pos = _grid_pos_embed(LAT_T // GRID_W).astype(x_sample.dtype)

    w_mix = mix_w_in[0]
    glu_off = N_RWKV_COLS
    gate_off = N_RWKV_COLS + 2 * D
    w_rwkv = w_mix[:, :glu_off].astype(BF16)
    w_glu = w_mix[:, glu_off:gate_off].astype(BF16)
    w_gate = w_mix[:, gate_off:].astype(BF16)

    x1 = _ffn_first_call(x_prompt.reshape(N_CTX, D), x_sample.reshape(LAT_B * LAT_T, D), pos, modss,
                         row(norm_ffn1[0]), ffn1_w_in[0].astype(BF16), ffn1_w_out[0].astype(BF16))

    u = _mixin_rwkv_call(x1, modss, row(norm_mix[0]), w_rwkv, row(rwkv_mu[0]))
    z, ga, gb = _mixin_gate_call(x1, modss, row(norm_mix[0]), w_glu, w_gate)

    ii = lax.broadcasted_iota(jnp.int32, (STAT_W, STAT_W), 0) // HEAD_DIM
    jj = lax.broadcasted_iota(jnp.int32, (STAT_W, STAT_W), 1) // HEAD_DIM
    bd = jnp.where(ii == jj, 1.0, 0.0).astype(BF16)

    y_att, s_f, s_b = _scan_call(
        u, state_rwkv_fwd[:, 0], state_rwkv_bwd[:, 0],
        rwkv_w0[0].reshape(2, 1, D), rwkv_w_up[0].astype(BF16), rwkv_a0[0].reshape(2, 1, D),
        rwkv_a_up[0].astype(BF16), row(rwkv_k_k[0]), row(rwkv_k_a[0]), row(rwkv_r_k[0]), bd,
        row(rwkv_lnx_w[0]), row(rwkv_lnx_b[0]))

    yconv = _conv_call(z, conv_dw[0], row(conv_db[0]), row(conv_ln_w[0]), row(conv_ln_b[0]))

    x2 = _mixout_call(x1, modss, y_att, u, ga, gb, yconv, rwkv_g_up[0].astype(BF16),
                      mix_w_out[0].astype(BF16))

    y_ctx, y_lat = _ffn_last_call(x2, modss, row(norm_ffn2[0]), ffn2_w_in[0].astype(BF16),
                                  ffn2_w_out[0].astype(BF16), row(norm_final))

    st = (CTX_B, 1, N_HEADS, HEAD_DIM, HEAD_DIM)
    return (y_ctx.reshape(CTX_B, CTX_T, D), y_lat.reshape(LAT_B, LAT_T, D),
            s_f.reshape(st).astype(x_prompt.dtype), s_b.reshape(st).astype(x_prompt.dtype))
```

```python
import functools
import itertools
import math

import jax
import jax.numpy as jnp
from jax import lax
from jax.experimental import pallas as pl
from jax.experimental.pallas import tpu as pltpu

F32 = jnp.float32
BF16 = jnp.bfloat16

D = 1024
N_HEADS = 16
HEAD_DIM = 64
D_FF = 4 * D
N_MOD = 9
CONV_WIDTH = 31
CONV_HALF = CONV_WIDTH // 2
LORA_W = 64
LORA_A = 64
LORA_G = 128
N_RWKV_COLS = 3 * D + 2 * LORA_W + 2 * LORA_A + LORA_G
N_RWKV_PAD = 3584
GRID_W = 64
EPS = 1e-6
LN_EPS = 1e-5
GN_EPS = 64e-5
DECAY_SCALE = 0.606531
POS_MAX_PERIOD = 10000.0

CTX_B, CTX_T = 32, 256
LAT_B, LAT_T = 4, 2048
N_CTX = CTX_B * CTX_T
N_TOK = N_CTX + LAT_B * LAT_T
SS_LEN = 2048
N_SS = N_TOK // SS_LEN
N_CTX_SS = N_CTX // SS_LEN

CHUNK = 64
HEAD_GROUP = 2
CTX_CHUNKS = CTX_T // CHUNK
LAT_CHUNKS = LAT_T // CHUNK
CTX_STEPS = CTX_B * CTX_CHUNKS
LAT_STEPS = LAT_B * LAT_CHUNKS
STAT_W = 256
SUBLANES = 8

VMEM_LIMIT = 56 * 1024 * 1024


def _cparams(sem):
    return pltpu.CompilerParams(dimension_semantics=sem, vmem_limit_bytes=VMEM_LIMIT)


def _sigmoid(x):
    return 1.0 / (1.0 + jnp.exp(-x))


def _dot(a, b):
    return jnp.dot(a, b, preferred_element_type=F32)


def _split_bf16(x):
    hi = x.astype(BF16)
    lo = (x - hi.astype(F32)).astype(BF16)
    return hi, lo


def _head_stat(x, bd):
    groups = D // STAT_W
    stacked = jnp.concatenate([x[:, g * STAT_W:(g + 1) * STAT_W] for g in range(groups)], axis=0)
    s = _dot(stacked.astype(BF16), bd)
    rows = x.shape[0]
    return jnp.concatenate([s[g * rows:(g + 1) * rows] for g in range(groups)], axis=1)


def _mod_kernel(cond_ref, w_ref, b_ref, o_ref):
    c = cond_ref[...]
    s = (c * _sigmoid(c)).astype(BF16)
    o_ref[...] = _dot(s, w_ref[...].astype(BF16)) + b_ref[...]


def _mod_call(cond, w_mod, b_mod):
    tn = 1152
    n = N_MOD * D
    return pl.pallas_call(
        _mod_kernel,
        out_shape=jax.ShapeDtypeStruct((8, n), F32),
        grid=(n // tn,),
        in_specs=[
            pl.BlockSpec((8, D), lambda j: (0, 0)),
            pl.BlockSpec((D, tn), lambda j: (0, j)),
            pl.BlockSpec((1, tn), lambda j: (0, j)),
        ],
        out_specs=pl.BlockSpec((8, tn), lambda j: (0, j)),
        compiler_params=_cparams(("arbitrary",)),
        name="mod",
    )(cond, w_mod, b_mod)


def _rms(x, gain):
    return x * lax.rsqrt(jnp.mean(x * x, axis=-1, keepdims=True) + EPS) * gain


def _ffn_start(x, mod_ref, gain_ref, h_scr, acc_scr, mod_row):
    y = _rms(x, gain_ref[...])
    h = y * (1.0 + mod_ref[mod_row + 1:mod_row + 2, :]) + mod_ref[mod_row:mod_row + 1, :]
    h_scr[...] = h.astype(BF16)
    acc_scr[...] = jnp.zeros_like(acc_scr)


def _ffn_step(wg_ref, wu_ref, wo_ref, h_scr, acc_scr):
    h = h_scr[...]
    g = _dot(h, wg_ref[...])
    u = _dot(h, wu_ref[...])
    a = (g * _sigmoid(g) * u).astype(BF16)
    acc_scr[...] += _dot(a, wo_ref[...])


def _ffn_first_kernel(xc_ref, xl_ref, pos_ref, mod_ref, gain_ref, wg_ref, wu_ref, wo_ref, o_ref,
                      h_scr, acc_scr, *, n_f, ctx_tiles):
    i = pl.program_id(0)
    f = pl.program_id(1)
    is_ctx = i < ctx_tiles
    sources = ((is_ctx, lambda: xc_ref[...]), (jnp.logical_not(is_ctx), lambda: xl_ref[...] + pos_ref[...]))

    for cond, x_in in sources:
        @pl.when(jnp.logical_and(f == 0, cond))
        def _():
            _ffn_start(x_in(), mod_ref, gain_ref, h_scr, acc_scr, 0)

    _ffn_step(wg_ref, wu_ref, wo_ref, h_scr, acc_scr)

    for cond, x_in in sources:
        @pl.when(jnp.logical_and(f == n_f - 1, cond))
        def _():
            o_ref[...] = x_in() + mod_ref[2:3, :] * (0.5 * acc_scr[...])


def _ffn_last_kernel(x_ref, mod_ref, gain_ref, wg_ref, wu_ref, wo_ref, gain2_ref, oc_ref, ol_ref,
                     h_scr, acc_scr, *, n_f, ctx_tiles):
    i = pl.program_id(0)
    f = pl.program_id(1)

    @pl.when(f == 0)
    def _():
        _ffn_start(x_ref[...], mod_ref, gain_ref, h_scr, acc_scr, 6)

    _ffn_step(wg_ref, wu_ref, wo_ref, h_scr, acc_scr)

    def result():
        return _rms(x_ref[...] + mod_ref[8:9, :] * (0.5 * acc_scr[...]), gain2_ref[...])

    @pl.when(jnp.logical_and(f == n_f - 1, i < ctx_tiles))
    def _():
        oc_ref[...] = result()

    @pl.when(jnp.logical_and(f == n_f - 1, i >= ctx_tiles))
    def _():
        ol_ref[...] = result()


def _ffn_specs(tm, tf):
    n_f = D_FF // tf
    ss_per = SS_LEN // tm
    return [
        pl.BlockSpec((None, N_MOD, D), lambda i, f: (i // ss_per, 0, 0)),
        pl.BlockSpec((1, D), lambda i, f: (0, 0)),
        pl.BlockSpec((D, tf), lambda i, f: (0, f)),
        pl.BlockSpec((D, tf), lambda i, f: (0, n_f + f)),
        pl.BlockSpec((tf, D), lambda i, f: (f, 0)),
    ]


def _ffn_first_call(xc, xl, pos, modss, gain, w_in, w_out, *, tm=1024, tf=512):
    n_f = D_FF // tf
    ctx_tiles = N_CTX // tm
    pos_tiles = LAT_T // tm
    kern = functools.partial(_ffn_first_kernel, n_f=n_f, ctx_tiles=ctx_tiles)
    return pl.pallas_call(
        kern,
        out_shape=jax.ShapeDtypeStruct((N_TOK, D), F32),
        grid=(N_TOK // tm, n_f),
        in_specs=[
            pl.BlockSpec((tm, D), lambda i, f: (jnp.minimum(i, ctx_tiles - 1), 0)),
            pl.BlockSpec((tm, D), lambda i, f: (jnp.maximum(i - ctx_tiles, 0), 0)),
            pl.BlockSpec((tm, D), lambda i, f: (jnp.maximum(i - ctx_tiles, 0) % pos_tiles, 0)),
        ] + _ffn_specs(tm, tf),
        out_specs=pl.BlockSpec((tm, D), lambda i, f: (i, 0)),
        scratch_shapes=[pltpu.VMEM((tm, D), BF16), pltpu.VMEM((tm, D), F32)],
        compiler_params=_cparams(("parallel", "arbitrary")),
        name="ffn_first",
    )(xc, xl, pos, modss, gain, w_in, w_in, w_out)


def _ffn_last_call(x, modss, gain, w_in, w_out, gain2, *, tm=1024, tf=512):
    n_f = D_FF // tf
    ctx_tiles = N_CTX // tm
    kern = functools.partial(_ffn_last_kernel, n_f=n_f, ctx_tiles=ctx_tiles)
    return pl.pallas_call(
        kern,
        out_shape=(jax.ShapeDtypeStruct((N_CTX, D), F32), jax.ShapeDtypeStruct((N_TOK - N_CTX, D), F32)),
        grid=(N_TOK // tm, n_f),
        in_specs=[pl.BlockSpec((tm, D), lambda i, f: (i, 0))] + _ffn_specs(tm, tf) + [
            pl.BlockSpec((1, D), lambda i, f: (0, 0))],
        out_specs=(pl.BlockSpec((tm, D), lambda i, f: (jnp.minimum(i, ctx_tiles - 1), 0)),
                   pl.BlockSpec((tm, D), lambda i, f: (jnp.maximum(i - ctx_tiles, 0), 0))),
        scratch_shapes=[pltpu.VMEM((tm, D), BF16), pltpu.VMEM((tm, D), F32)],
        compiler_params=_cparams(("parallel", "arbitrary")),
        name="ffn_last",
    )(x, modss, gain, w_in, w_in, w_out, gain2)


def _mixin_rwkv_kernel(x_ref, mod_ref, gain_ref, w_ref, mu_ref, keep_prev_ref, keep_next_ref, o_ref, h_scr):
    j = pl.program_id(1)

    @pl.when(j == 0)
    def _():
        y = _rms(x_ref[...], gain_ref[...])
        h_scr[...] = (y * (1.0 + mod_ref[4:5, :]) + mod_ref[3:4, :]).astype(BF16)

    u = _dot(h_scr[...], w_ref[...])
    rows, cols = u.shape
    keep_prev = keep_prev_ref[...]
    keep_next = keep_next_ref[...]
    lanes = keep_prev.shape[1]
    for l in range(cols // lanes):
        ls = slice(l * lanes, (l + 1) * lanes)
        ul = u[:, ls]
        neighbours = pltpu.roll(ul, 1, 0) * keep_prev + pltpu.roll(ul, rows - 1, 0) * keep_next
        mu = mu_ref[:, ls]
        o_ref[:, ls] = ul * (1.0 - mu) + neighbours * (0.5 * mu)


def _mixin_rwkv_call(x, modss, gain, w, mu, *, tn=512, lanes=128):
    tm = SS_LEN
    pos = lax.broadcasted_iota(jnp.int32, (2, tm, lanes), 1)
    seq_len = jnp.array([CTX_T, LAT_T], jnp.int32)[:, None, None]
    pos = pos & (seq_len - 1)
    keep_prev = jnp.where(pos == 0, 0.0, 1.0)
    keep_next = jnp.where(pos == seq_len - 1, 0.0, 1.0)
    kind = lambda i, j: (jnp.where(i < N_CTX_SS, 0, 1), 0, 0)
    return pl.pallas_call(
        _mixin_rwkv_kernel,
        out_shape=jax.ShapeDtypeStruct((N_TOK, N_RWKV_PAD), F32),
        grid=(N_TOK // tm, N_RWKV_PAD // tn),
        in_specs=[
            pl.BlockSpec((tm, D), lambda i, j: (i, 0)),
            pl.BlockSpec((None, N_MOD, D), lambda i, j: (i, 0, 0)),
            pl.BlockSpec((1, D), lambda i, j: (0, 0)),
            pl.BlockSpec((D, tn), lambda i, j: (0, j)),
            pl.BlockSpec((1, tn), lambda i, j: (0, j)),
            pl.BlockSpec((None, tm, lanes), kind),
            pl.BlockSpec((None, tm, lanes), kind),
        ],
        out_specs=pl.BlockSpec((tm, tn), lambda i, j: (i, j)),
        scratch_shapes=[pltpu.VMEM((tm, D), BF16)],
        compiler_params=_cparams(("parallel", "arbitrary")),
        name="mixin_rwkv",
    )(x, modss, gain, w, mu, keep_prev, keep_next)


def _mixin_gate_kernel(x_ref, mod_ref, gain_ref, wa_ref, wb_ref, wga_ref, wgb_ref, z_ref, ga_ref, gb_ref,
                       h_scr):
    j = pl.program_id(1)

    @pl.when(j == 0)
    def _():
        y = _rms(x_ref[...], gain_ref[...])
        h_scr[...] = (y * (1.0 + mod_ref[4:5, :]) + mod_ref[3:4, :]).astype(BF16)

    h = h_scr[...]
    z_ref[...] = _dot(h, wa_ref[...]) * _sigmoid(_dot(h, wb_ref[...]))
    ga_ref[...] = _sigmoid(_dot(h, wga_ref[...]))
    gb_ref[...] = _sigmoid(_dot(h, wgb_ref[...]))


def _mixin_gate_call(x, modss, gain, w_glu, w_gate, *, tm=1024, tn=256):
    n_j = D // tn
    ss_per = SS_LEN // tm
    sds = jax.ShapeDtypeStruct((N_TOK, D), F32)
    lo = lambda i, j: (0, j)
    hi = lambda i, j: (0, n_j + j)
    out = pl.BlockSpec((tm, tn), lambda i, j: (i, j))
    return pl.pallas_call(
        _mixin_gate_kernel,
        out_shape=(sds, sds, sds),
        grid=(N_TOK // tm, n_j),
        in_specs=[
            pl.BlockSpec((tm, D), lambda i, j: (i, 0)),
            pl.BlockSpec((None, N_MOD, D), lambda i, j: (i // ss_per, 0, 0)),
            pl.BlockSpec((1, D), lambda i, j: (0, 0)),
            pl.BlockSpec((D, tn), lo),
            pl.BlockSpec((D, tn), hi),
            pl.BlockSpec((D, tn), lo),
            pl.BlockSpec((D, tn), hi),
        ],
        out_specs=(out, out, out),
        scratch_shapes=[pltpu.VMEM((tm, D), BF16)],
        compiler_params=_cparams(("parallel", "arbitrary")),
        name="mixin_gate",
    )(x, modss, gain, w_glu, w_glu, w_gate, w_gate)


def _scan_prep(x_ref, d, reverse, prm, bank):
    (w0_ref, wup_ref, a0_ref, aup_ref, kk_ref, ka_ref, rk_ref, bd_ref) = prm
    c = CHUNK
    wd = x_ref[:, 3 * D + d * LORA_W:3 * D + (d + 1) * LORA_W]
    ad_off = 3 * D + 2 * LORA_W
    ad = x_ref[:, ad_off + d * LORA_A:ad_off + (d + 1) * LORA_A]
    logw = -DECAY_SCALE * _sigmoid(w0_ref[d] + _dot(jnp.tanh(wd).astype(BF16), wup_ref[d]))
    a = _sigmoid(a0_ref[d] + _dot(ad.astype(BF16), aup_ref[d]))
    yield

    k = x_ref[:, D:2 * D]
    kk = k * kk_ref[...]
    kk = kk * lax.rsqrt(_head_stat(kk * kk, bd_ref[...]) + 1e-12)
    kd = k * (1.0 + (a - 1.0) * ka_ref[...])
    b = kk * a
    yield

    v = x_ref[:, 2 * D:3 * D]
    bank["v"][d] = v
    bank["bonus"][d] = _head_stat(x_ref[:, 0:D] * kd * rk_ref[...], bd_ref[...]) * v
    ti = lax.broadcasted_iota(jnp.int32, (c, c), 0)
    si = lax.broadcasted_iota(jnp.int32, (c, c), 1)
    tri = jnp.where((ti <= si) if reverse else (ti >= si), 1.0, 0.0).astype(BF16)
    lw_hi, lw_lo = _split_bf16(logw)
    cum = _dot(tri, lw_hi) + _dot(tri, lw_lo)
    yield

    tot = cum[0:1, :] if reverse else cum[c - 1:c, :]
    p_inv = jnp.exp(-cum)
    p_tot = jnp.exp(tot)
    r_b = b * p_inv
    r_k = kd * p_inv
    bank["p_tot"][d] = p_tot
    bank["r_b"][d] = r_b
    bank["r_k"][d] = r_k
    bank["s_b"][d] = (r_b * p_tot).astype(BF16)
    bank["s_k"][d] = (r_k * p_tot).astype(BF16)
    bank["l_kk"][d] = (kk * jnp.exp(cum - logw)).astype(BF16)
    bank["l_r"][d] = (x_ref[:, 0:D] * jnp.exp(cum)).astype(BF16)
    yield


def _group_norm(y, bd, lnw, lnb):
    yc = y - _head_stat(y, bd) * (1.0 / HEAD_DIM)
    var = _head_stat(yc * yc, bd) * (1.0 / HEAD_DIM)
    return yc * lax.rsqrt(var + GN_EPS) * lnw + lnb


_BANK_F32 = ("r_b", "r_k", "v", "bonus")
_BANK_BF16 = ("l_kk", "l_r", "s_b", "s_k")
_BANK_NAMES = _BANK_F32 + _BANK_BF16 + ("p_tot",)


def _scan_kernel(x0f_ref, x0b_ref, xnf_ref, xnb_ref, s0f_ref, s0b_ref, w0_ref, wup_ref, a0_ref, aup_ref,
                 kk_ref, ka_ref, rk_ref, bd_ref, lnw_ref, lnb_ref, y_ref, sf_ref, sb_ref, s_scr, bonus_scr,
                 *bank_refs):
    s = pl.program_id(0)
    is_ctx = s < CTX_STEPS
    n_chunks = jnp.where(is_ctx, CTX_CHUNKS, LAT_CHUNKS)
    t_step = jnp.where(is_ctx, s, s - CTX_STEPS)
    cidx = t_step & (n_chunks - 1)
    seq_in_ss = jnp.where(is_ctx, (t_step >> (CTX_CHUNKS.bit_length() - 1)) & (SS_LEN // CTX_T - 1), 0)
    row_f = pl.multiple_of((seq_in_ss * n_chunks + cidx) * CHUNK, CHUNK)
    row_b = pl.multiple_of((seq_in_ss * n_chunks + n_chunks - 1 - cidx) * CHUNK, CHUNK)
    second_visit = cidx >= (n_chunks >> 1)
    prm = (w0_ref, wup_ref, a0_ref, aup_ref, kk_ref, ka_ref, rk_ref, bd_ref)
    c, g, gw = CHUNK, HEAD_GROUP, HEAD_GROUP * HEAD_DIM
    n_groups = N_HEADS // g
    s0_refs = (s0f_ref, s0b_ref)
    n_bank = len(_BANK_NAMES)
    banks = (dict(zip(_BANK_NAMES, bank_refs[:n_bank])), dict(zip(_BANK_NAMES, bank_refs[n_bank:])))

    @pl.when(s == 0)
    def _():
        for _ in _scan_prep(x0f_ref, 0, False, prm, banks[0]):
            pass
        for _ in _scan_prep(x0b_ref, 1, True, prm, banks[0]):
            pass

    @pl.when(jnp.logical_and(cidx == 0, is_ctx))
    def _():
        s_scr[...] = jnp.zeros_like(s_scr)

    @pl.when(jnp.logical_and(cidx == 0, jnp.logical_not(is_ctx)))
    def _():
        for d in range(2):
            for gi in range(n_groups):
                for l in range(g):
                    pieces = [jnp.zeros((HEAD_DIM, HEAD_DIM), F32)] * g
                    pieces[l] = s0_refs[d][gi * g + l]
                    s_scr[d, gi, l * HEAD_DIM:(l + 1) * HEAD_DIM, :] = jnp.concatenate(pieces, axis=1)

    def step(cur, nxt):
        row = lax.broadcasted_iota(jnp.int32, (c, gw), 0)
        lane = lax.broadcasted_iota(jnp.int32, (c, gw), 1)
        col = lane & (HEAD_DIM - 1)
        head_of_lane = lane >> 6
        eye = jnp.where(row == col, 1.0, 0.0)
        strict = (row > col, row < col)
        incl = (row >= col, row <= col)
        srow = lax.broadcasted_iota(jnp.int32, (gw, gw), 0) >> 6
        slane = lax.broadcasted_iota(jnp.int32, (gw, gw), 1) >> 6
        state_mask = srow == slane
        nt = (((1,), (1,)), ((), ()))
        tn = (((0,), (0,)), ((), ()))

        def blockdiag(x):
            return jnp.concatenate([jnp.where(head_of_lane == l, x, 0.0) for l in range(g)], axis=0).astype(BF16)

        def same_block(n):
            shift = n.bit_length() - 1
            return (row >> shift) == (col >> shift)

        items = [(d, gi) for d in range(2) for gi in range(n_groups)]
        n_items = len(items)
        cols = lambda gi: slice(gi * gw, (gi + 1) * gw)
        get = lambda name, d, gi: cur[name][d, :, cols(gi)]

        prep_next = itertools.chain(_scan_prep(xnf_ref, 0, False, prm, nxt), _scan_prep(xnb_ref, 1, True, prm, nxt))

        s_old = [s_scr[d, gi] for d, gi in items]
        lhs = [jnp.concatenate([get("l_kk", d, gi), get("l_r", d, gi)], axis=0) for d, gi in items]
        rt = [jnp.concatenate([blockdiag(get("r_b", d, gi)), blockdiag(get("r_k", d, gi))], axis=0)
              for d, gi in items]
        amat = [lax.dot_general(lhs[i], rt[i], nt, preferred_element_type=F32) for i in range(n_items)]
        ls = [lax.dot_general(lhs[i], s_old[i].astype(BF16), nt, preferred_element_type=F32) for i in range(n_items)]
        a_ab = [jnp.where(strict[d], amat[i][:c, :gw], 0.0) for i, (d, gi) in enumerate(items)]
        a_ak = [jnp.where(strict[d], amat[i][:c, gw:], 0.0).astype(BF16) for i, (d, gi) in enumerate(items)]
        a_r = [jnp.concatenate([jnp.where(incl[d], amat[i][c:, :gw], 0.0),
                                jnp.where(incl[d], amat[i][c:, gw:], 0.0)], axis=1).astype(BF16)
               for i, (d, gi) in enumerate(items)]
        bd_v = [blockdiag(get("v", d, gi)) for d, gi in items]
        rhs = [-(ls[i][:c] + _dot(a_ak[i], bd_v[i])) for i in range(n_items)]
        next(prep_next, None)

        tinv = [eye - jnp.where(same_block(2), a, 0.0) for a in a_ab]
        n = 2
        while n < c:
            lower = jnp.logical_and(same_block(2 * n), jnp.logical_not(same_block(n)))
            off = [jnp.where(lower, a, 0.0).astype(BF16) for a in a_ab]
            w = [_dot(off[i], blockdiag(tinv[i])) for i in range(n_items)]
            tinv = [tinv[i] - _dot(tinv[i].astype(BF16), blockdiag(w[i])) for i in range(n_items)]
            next(prep_next, None)
            n *= 2

        u = [_dot(tinv[i].astype(BF16), blockdiag(rhs[i])) for i in range(n_items)]
        next(prep_next, None)
        y = [ls[i][c:] + _dot(a_r[i], jnp.concatenate([blockdiag(u[i]), bd_v[i]], axis=0)) for i in range(n_items)]
        next(prep_next, None)
        for i, (d, gi) in enumerate(items):
            uv = jnp.concatenate([u[i].astype(BF16), get("v", d, gi).astype(BF16)], axis=0)
            sbk = jnp.concatenate([get("s_b", d, gi), get("s_k", d, gi)], axis=0)
            upd = lax.dot_general(uv, sbk, tn, preferred_element_type=F32)
            s_scr[d, gi] = s_old[i] * cur["p_tot"][d, :, cols(gi)] + jnp.where(state_mask, upd, 0.0)
        for _ in prep_next:
            pass

        visits = ((row_f, jnp.concatenate(y[:n_groups], axis=1), cur["bonus"][0]),
                  (row_b, jnp.concatenate(y[n_groups:], axis=1), cur["bonus"][1]))

        @pl.when(jnp.logical_not(second_visit))
        def _():
            for rows, y_d, bonus_d in visits:
                y_ref[pl.ds(rows, c), :] = y_d
                bonus_scr[pl.ds(rows, c), :] = bonus_d

        @pl.when(second_visit)
        def _():
            for rows, y_d, bonus_d in visits:
                y_sum = y_ref[pl.ds(rows, c), :] + y_d
                y_ref[pl.ds(rows, c), :] = (_group_norm(y_sum, bd_ref[...], lnw_ref[...], lnb_ref[...])
                                             + bonus_scr[pl.ds(rows, c), :] + bonus_d)

    parity = s & 1

    @pl.when(parity == 0)
    def _():
        step(banks[0], banks[1])

    @pl.when(parity == 1)
    def _():
        step(banks[1], banks[0])

    @pl.when(jnp.logical_and(cidx == n_chunks - 1, is_ctx))
    def _():
        for d, out_ref in enumerate((sf_ref, sb_ref)):
            for h in range(N_HEADS):
                gi, l = divmod(h, g)
                hs = slice(l * HEAD_DIM, (l + 1) * HEAD_DIM)
                out_ref[h] = s_scr[d, gi, hs, hs]


def _scan_row_block(s, reverse):
    ctx = s < CTX_STEPS
    n_chunks = jnp.where(ctx, CTX_CHUNKS, LAT_CHUNKS)
    t = jnp.where(ctx, s, s - CTX_STEPS)
    cidx = t & (n_chunks - 1)
    if reverse:
        cidx = n_chunks - 1 - cidx
    base = jnp.where(ctx, 0, CTX_STEPS)
    return base + (t - (t & (n_chunks - 1))) + cidx


def _scan_call(u, s0f, s0b, w0, wup, a0, aup, k_k, k_a, r_k, bd, lnw, lnb):
    assert CTX_STEPS // N_CTX_SS == LAT_CHUNKS, "context and latent super sequences take the same number of steps"
    steps_per_ss = LAT_CHUNKS
    n_steps = CTX_STEPS + LAT_STEPS
    first_f = lambda s: (_scan_row_block(s * 0, False), 0)
    first_b = lambda s: (_scan_row_block(s * 0, True), 0)
    next_f = lambda s: (_scan_row_block(jnp.minimum(s + 1, n_steps - 1), False), 0)
    next_b = lambda s: (_scan_row_block(jnp.minimum(s + 1, n_steps - 1), True), 0)
    lat_seq = lambda s: (jnp.maximum(s - CTX_STEPS, 0) // LAT_CHUNKS, 0, 0, 0)
    ctx_seq = lambda s: (jnp.minimum(s // CTX_CHUNKS, CTX_B - 1), 0, 0, 0)
    const2 = lambda s: (0, 0)
    const3 = lambda s: (0, 0, 0)
    st_block = (None, N_HEADS, HEAD_DIM, HEAD_DIM)
    y_sds = jax.ShapeDtypeStruct((N_TOK, D), F32)
    s_sds = jax.ShapeDtypeStruct((CTX_B, N_HEADS, HEAD_DIM, HEAD_DIM), F32)
    return pl.pallas_call(
        _scan_kernel,
        out_shape=(y_sds, s_sds, s_sds),
        grid=(n_steps,),
        in_specs=[
            pl.BlockSpec((CHUNK, N_RWKV_COLS), first_f),
            pl.BlockSpec((CHUNK, N_RWKV_COLS), first_b),
            pl.BlockSpec((CHUNK, N_RWKV_COLS), next_f),
            pl.BlockSpec((CHUNK, N_RWKV_COLS), next_b),
            pl.BlockSpec(st_block, lat_seq),
            pl.BlockSpec(st_block, lat_seq),
            pl.BlockSpec((2, 1, D), const3),
            pl.BlockSpec((2, LORA_W, D), const3),
            pl.BlockSpec((2, 1, D), const3),
            pl.BlockSpec((2, LORA_A, D), const3),
            pl.BlockSpec((1, D), const2),
            pl.BlockSpec((1, D), const2),
            pl.BlockSpec((1, D), const2),
            pl.BlockSpec((STAT_W, STAT_W), const2),
            pl.BlockSpec((1, D), const2),
            pl.BlockSpec((1, D), const2),
        ],
        out_specs=(
            pl.BlockSpec((SS_LEN, D), lambda s: (s // steps_per_ss, 0)),
            pl.BlockSpec(st_block, ctx_seq),
            pl.BlockSpec(st_block, ctx_seq),
        ),
        scratch_shapes=[
            pltpu.VMEM((2, N_HEADS // HEAD_GROUP, HEAD_GROUP * HEAD_DIM, HEAD_GROUP * HEAD_DIM), F32),
            pltpu.VMEM((SS_LEN, D), F32),
        ] + 2 * ([pltpu.VMEM((2, CHUNK, D), F32)] * len(_BANK_F32) + [pltpu.VMEM((2, CHUNK, D), BF16)] * len(_BANK_BF16)
                 + [pltpu.VMEM((2, 1, D), F32)]),
        compiler_params=_cparams(("arbitrary",)),
        name="wkv7_scan",
    )(u, u, u, u, s0f, s0b, w0, wup, a0, aup, k_k, k_a, r_k, bd, lnw, lnb)


def _conv_kernel(z_ref, zp_ref, zn_ref, w_ref, b_ref, lnw_ref, lnb_ref, o_ref, pad_scr, *, tm, halo):
    i = pl.program_id(0)
    tiles_per_lat = LAT_T // tm
    j = (i - N_CTX // tm) % tiles_per_lat
    is_ctx = i < N_CTX // tm
    keep_prev = jnp.where(jnp.logical_or(is_ctx, j == 0), 0.0, 1.0)
    keep_next = jnp.where(jnp.logical_or(is_ctx, j == tiles_per_lat - 1), 0.0, 1.0)
    pad_scr[0, 0:halo, :] = zp_ref[...] * keep_prev
    pad_scr[0, halo:halo + tm, :] = z_ref[...]
    pad_scr[0, halo + tm:halo + tm + halo, :] = zn_ref[...] * keep_next
    span = tm + 2 * halo - SUBLANES
    for m in range(1, SUBLANES):
        pad_scr[m, 0:span, :] = pad_scr[0, m:m + span, :]

    rb = 128
    lanes = 128

    def lane_block(cb, carry):
        cs = pl.ds(pl.multiple_of(cb * lanes, lanes), lanes)
        for r0 in range(0, tm, rb):
            acc = jnp.broadcast_to(b_ref[:, cs], (rb, lanes))
            for tap in range(CONV_WIDTH):
                q, m = divmod(tap + halo - CONV_HALF, SUBLANES)
                start = r0 + q * SUBLANES
                acc = acc + w_ref[tap:tap + 1, cs] * pad_scr[m, start:start + rb, cs]
            o_ref[r0:r0 + rb, cs] = acc
        return carry

    lax.fori_loop(0, D // lanes, lane_block, 0)
    y = o_ref[...]
    mean = jnp.mean(y, axis=-1, keepdims=True)
    yc = y - mean
    var = jnp.mean(yc * yc, axis=-1, keepdims=True)
    y = yc * lax.rsqrt(var + LN_EPS) * lnw_ref[...] + lnb_ref[...]
    o_ref[...] = y * _sigmoid(y)


def _conv_call(z, w, b, lnw, lnb, *, tm=256, halo=16):
    hb = tm // halo
    n_h = N_TOK // halo
    kern = functools.partial(_conv_kernel, tm=tm, halo=halo)
    return pl.pallas_call(
        kern,
        out_shape=jax.ShapeDtypeStruct((N_TOK, D), F32),
        grid=(N_TOK // tm,),
        in_specs=[
            pl.BlockSpec((tm, D), lambda i: (i, 0)),
            pl.BlockSpec((halo, D), lambda i: (jnp.maximum(i * hb - 1, 0), 0)),
            pl.BlockSpec((halo, D), lambda i: (jnp.minimum((i + 1) * hb, n_h - 1), 0)),
            pl.BlockSpec((CONV_WIDTH, D), lambda i: (0, 0)),
            pl.BlockSpec((1, D), lambda i: (0, 0)),
            pl.BlockSpec((1, D), lambda i: (0, 0)),
            pl.BlockSpec((1, D), lambda i: (0, 0)),
        ],
        out_specs=pl.BlockSpec((tm, D), lambda i: (i, 0)),
        scratch_shapes=[pltpu.VMEM((SUBLANES, tm + 2 * halo, D), F32)],
        compiler_params=_cparams(("parallel",)),
        name="conv_branch",
    )(z, z, z, w, b, lnw, lnb)


def _mixout_kernel(x_ref, mod_ref, y_ref, gd_ref, ga_ref, gb_ref, yc_ref, gup_ref, wo_ref, o_ref):
    g = _dot(_sigmoid(gd_ref[...]).astype(BF16), gup_ref[...])
    mixed = ga_ref[...] * (y_ref[...] * g) + gb_ref[...] * yc_ref[...]
    o_ref[...] = x_ref[...] + mod_ref[5:6, :] * _dot(mixed.astype(BF16), wo_ref[...])


def _mixout_call(x, modss, y_att, u, ga, gb, yconv, g_up, w_out, *, tm=512):
    ss_per = SS_LEN // tm
    tok = lambda i: (i, 0)
    const2 = lambda i: (0, 0)
    gd_blk = (3 * D + 2 * LORA_W + 2 * LORA_A) // LORA_G
    return pl.pallas_call(
        _mixout_kernel,
        out_shape=jax.ShapeDtypeStruct((N_TOK, D), F32),
        grid=(N_TOK // tm,),
        in_specs=[
            pl.BlockSpec((tm, D), tok),
            pl.BlockSpec((None, N_MOD, D), lambda i: (i // ss_per, 0, 0)),
            pl.BlockSpec((tm, D), tok),
            pl.BlockSpec((tm, LORA_G), lambda i: (i, gd_blk)),
            pl.BlockSpec((tm, D), tok),
            pl.BlockSpec((tm, D), tok),
            pl.BlockSpec((tm, D), tok),
            pl.BlockSpec((LORA_G, D), const2),
            pl.BlockSpec((D, D), const2),
        ],
        out_specs=pl.BlockSpec((tm, D), tok),
        compiler_params=_cparams(("parallel",)),
        name="mixout",
    )(x, modss, y_att, u, ga, gb, yconv, g_up, w_out)


def _grid_pos_embed(rows):
    row, col = jnp.meshgrid(jnp.arange(rows, dtype=F32), jnp.arange(GRID_W, dtype=F32), indexing="ij")
    n_freq = D // 4
    freqs = jnp.exp(-math.log(POS_MAX_PERIOD) * jnp.arange(n_freq, dtype=F32) / n_freq)
    ang_r = row.reshape(-1, 1) * freqs
    ang_c = col.reshape(-1, 1) * freqs
    return jnp.concatenate([jnp.sin(ang_r), jnp.cos(ang_r), jnp.sin(ang_c), jnp.cos(ang_c)], axis=-1)


def kernel(x_prompt, x_sample, state_rwkv_fwd, state_rwkv_bwd, c, c_ctx, w_mod, b_mod, norm_ffn1, ffn1_w_in, ffn1_w_out, norm_mix, mix_w_in, rwkv_mu, rwkv_w0, rwkv_w_up, rwkv_a0, rwkv_a_up, rwkv_g_up, rwkv_k_k, rwkv_k_a, rwkv_r_k, rwkv_lnx_w, rwkv_lnx_b, conv_dw, conv_db, conv_ln_w, conv_ln_b, mix_w_out, norm_ffn2, ffn2_w_in, ffn2_w_out, norm_final):
    assert x_prompt.shape == (CTX_B, CTX_T, D) and x_sample.shape == (LAT_B, LAT_T, D)
    assert w_mod.shape[0] == 1, "single-layer trunk"
    row = lambda t: t.reshape(1, -1)

    cond = jnp.concatenate([c, c_ctx[None], jnp.zeros((8 - LAT_B - 1, D), F32)], axis=0)
    mod = _mod_call(cond, w_mod[0], row(b_mod[0])).reshape(8, N_MOD, D)
    ss_cond = jnp.array([LAT_B] * N_CTX_SS + list(range(LAT_B)), jnp.int32)
    modss = mod[ss_cond]

    pos = _grid_pos_embed(LAT_T // GRID_W).astype(x_sample.dtype)

    w_mix = mix_w_in[0]
    glu_off = N_RWKV_COLS
    gate_off = N_RWKV_COLS + 2 * D
    col_pad = ((0, 0), (0, N_RWKV_PAD - N_RWKV_COLS))
    w_rwkv = jnp.pad(w_mix[:, :glu_off], col_pad).astype(BF16)
    w_glu = w_mix[:, glu_off:gate_off].astype(BF16)
    w_gate = w_mix[:, gate_off:].astype(BF16)

    x1 = _ffn_first_call(x_prompt.reshape(N_CTX, D), x_sample.reshape(LAT_B * LAT_T, D), pos, modss,
                         row(norm_ffn1[0]), ffn1_w_in[0].astype(BF16), ffn1_w_out[0].astype(BF16))

    u = _mixin_rwkv_call(x1, modss, row(norm_mix[0]), w_rwkv, jnp.pad(row(rwkv_mu[0]), col_pad))
    z, ga, gb = _mixin_gate_call(x1, modss, row(norm_mix[0]), w_glu, w_gate)

    ii = lax.broadcasted_iota(jnp.int32, (STAT_W, STAT_W), 0) // HEAD_DIM
    jj = lax.broadcasted_iota(jnp.int32, (STAT_W, STAT_W), 1) // HEAD_DIM
    bd = jnp.where(ii == jj, 1.0, 0.0).astype(BF16)

    y_att, s_f, s_b = _scan_call(
        u, state_rwkv_fwd[:, 0], state_rwkv_bwd[:, 0],
        rwkv_w0[0].reshape(2, 1, D), rwkv_w_up[0].astype(BF16), rwkv_a0[0].reshape(2, 1, D),
        rwkv_a_up[0].astype(BF16), row(rwkv_k_k[0]), row(rwkv_k_a[0]), row(rwkv_r_k[0]), bd,
        row(rwkv_lnx_w[0]), row(rwkv_lnx_b[0]))

    yconv = _conv_call(z, conv_dw[0], row(conv_db[0]), row(conv_ln_w[0]), row(conv_ln_b[0]))

    x2 = _mixout_call(x1, modss, y_att, u, ga, gb, yconv, rwkv_g_up[0].astype(BF16),
                      mix_w_out[0].astype(BF16))

    y_ctx, y_lat = _ffn_last_call(x2, modss, row(norm_ffn2[0]), ffn2_w_in[0].astype(BF16),
                                  ffn2_w_out[0].astype(BF16), row(norm_final))

    st = (CTX_B, 1, N_HEADS, HEAD_DIM, HEAD_DIM)
    return (y_ctx.reshape(CTX_B, CTX_T, D), y_lat.reshape(LAT_B, LAT_T, D),
            s_f.reshape(st).astype(x_prompt.dtype), s_b.reshape(st).astype(x_prompt.dtype))
```

```python
import functools
import math

import jax
import jax.numpy as jnp
from jax import lax
from jax.experimental import pallas as pl
from jax.experimental.pallas import tpu as pltpu

F32 = jnp.float32
BF16 = jnp.bfloat16

D = 1024
N_HEADS = 16
HEAD_DIM = 64
D_FF = 4 * D
N_MOD = 9
CONV_WIDTH = 31
CONV_HALF = CONV_WIDTH // 2
LORA_W = 64
LORA_A = 64
LORA_G = 128
N_RWKV_COLS = 3 * D + 2 * LORA_W + 2 * LORA_A + LORA_G
N_RWKV_PAD = 3584
GRID_W = 64
EPS = 1e-6
LN_EPS = 1e-5
GN_EPS = 64e-5
DECAY_SCALE = 0.606531
POS_MAX_PERIOD = 10000.0

CTX_B, CTX_T = 32, 256
LAT_B, LAT_T = 4, 2048
N_CTX = CTX_B * CTX_T
N_TOK = N_CTX + LAT_B * LAT_T
SS_LEN = 2048
N_SS = N_TOK // SS_LEN
N_CTX_SS = N_CTX // SS_LEN

CHUNK = 64
HEAD_GROUP = 2
CTX_CHUNKS = CTX_T // CHUNK
LAT_CHUNKS = LAT_T // CHUNK
CTX_STEPS = CTX_B * CTX_CHUNKS
LAT_STEPS = LAT_B * LAT_CHUNKS
STAT_W = 256
SUBLANES = 8

VMEM_LIMIT = 58 * 1024 * 1024


def _cparams(sem):
    return pltpu.CompilerParams(dimension_semantics=sem, vmem_limit_bytes=VMEM_LIMIT)


def _sigmoid(x):
    return 1.0 / (1.0 + jnp.exp(-x))


def _dot(a, b):
    return jnp.dot(a, b, preferred_element_type=F32)


def _split_bf16(x):
    hi = x.astype(BF16)
    lo = (x - hi.astype(F32)).astype(BF16)
    return hi, lo


def _head_stat(x, bd):
    groups = D // STAT_W
    stacked = jnp.concatenate([x[:, g * STAT_W:(g + 1) * STAT_W] for g in range(groups)], axis=0)
    s = _dot(stacked.astype(BF16), bd)
    rows = x.shape[0]
    return jnp.concatenate([s[g * rows:(g + 1) * rows] for g in range(groups)], axis=1)


def _mod_kernel(cond_ref, w_ref, b_ref, o_ref):
    c = cond_ref[...]
    s = (c * _sigmoid(c)).astype(BF16)
    o_ref[...] = _dot(s, w_ref[...].astype(BF16)) + b_ref[...]


def _mod_call(cond, w_mod, b_mod):
    tn = 1152
    n = N_MOD * D
    return pl.pallas_call(
        _mod_kernel,
        out_shape=jax.ShapeDtypeStruct((8, n), F32),
        grid=(n // tn,),
        in_specs=[
            pl.BlockSpec((8, D), lambda j: (0, 0)),
            pl.BlockSpec((D, tn), lambda j: (0, j)),
            pl.BlockSpec((1, tn), lambda j: (0, j)),
        ],
        out_specs=pl.BlockSpec((8, tn), lambda j: (0, j)),
        compiler_params=_cparams(("arbitrary",)),
        name="mod",
    )(cond, w_mod, b_mod)


def _rms(x, gain):
    return x * lax.rsqrt(jnp.mean(x * x, axis=-1, keepdims=True) + EPS) * gain


def _ffn_start(x, mod_ref, gain_ref, h_scr, acc_scr, mod_row):
    y = _rms(x, gain_ref[...])
    h = y * (1.0 + mod_ref[mod_row + 1:mod_row + 2, :]) + mod_ref[mod_row:mod_row + 1, :]
    h_scr[...] = h.astype(BF16)
    acc_scr[...] = jnp.zeros_like(acc_scr)


def _ffn_step(wg_ref, wu_ref, wo_ref, h_scr, acc_scr):
    h = h_scr[...]
    g = _dot(h, wg_ref[...].astype(BF16))
    u = _dot(h, wu_ref[...].astype(BF16))
    a = (g * _sigmoid(g) * u).astype(BF16)
    acc_scr[...] += _dot(a, wo_ref[...].astype(BF16))


def _ffn_first_kernel(xc_ref, xl_ref, pos_ref, mod_ref, gain_ref, wg_ref, wu_ref, wo_ref, o_ref,
                      h_scr, acc_scr, *, n_f, ctx_tiles):
    i = pl.program_id(0)
    f = pl.program_id(1)
    is_ctx = i < ctx_tiles
    sources = ((is_ctx, lambda: xc_ref[...]), (jnp.logical_not(is_ctx), lambda: xl_ref[...] + pos_ref[...]))

    for cond, x_in in sources:
        @pl.when(jnp.logical_and(f == 0, cond))
        def _():
            _ffn_start(x_in(), mod_ref, gain_ref, h_scr, acc_scr, 0)

    _ffn_step(wg_ref, wu_ref, wo_ref, h_scr, acc_scr)

    for cond, x_in in sources:
        @pl.when(jnp.logical_and(f == n_f - 1, cond))
        def _():
            o_ref[...] = x_in() + mod_ref[2:3, :] * (0.5 * acc_scr[...])


def _ffn_last_kernel(x_ref, mod_ref, gain_ref, wg_ref, wu_ref, wo_ref, gain2_ref, oc_ref, ol_ref,
                     h_scr, acc_scr, *, n_f, ctx_tiles):
    i = pl.program_id(0)
    f = pl.program_id(1)

    @pl.when(f == 0)
    def _():
        _ffn_start(x_ref[...], mod_ref, gain_ref, h_scr, acc_scr, 6)

    _ffn_step(wg_ref, wu_ref, wo_ref, h_scr, acc_scr)

    def result():
        return _rms(x_ref[...] + mod_ref[8:9, :] * (0.5 * acc_scr[...]), gain2_ref[...])

    @pl.when(jnp.logical_and(f == n_f - 1, i < ctx_tiles))
    def _():
        oc_ref[...] = result()

    @pl.when(jnp.logical_and(f == n_f - 1, i >= ctx_tiles))
    def _():
        ol_ref[...] = result()


def _ffn_specs(tm, tf):
    n_f = D_FF // tf
    ss_per = SS_LEN // tm
    return [
        pl.BlockSpec((None, N_MOD, D), lambda i, f: (i // ss_per, 0, 0)),
        pl.BlockSpec((1, D), lambda i, f: (0, 0)),
        pl.BlockSpec((D, tf), lambda i, f: (0, f)),
        pl.BlockSpec((D, tf), lambda i, f: (0, n_f + f)),
        pl.BlockSpec((tf, D), lambda i, f: (f, 0)),
    ]


def _ffn_first_call(xc, xl, pos, modss, gain, w_in, w_out, *, tm=1024, tf=512):
    n_f = D_FF // tf
    ctx_tiles = N_CTX // tm
    pos_tiles = LAT_T // tm
    kern = functools.partial(_ffn_first_kernel, n_f=n_f, ctx_tiles=ctx_tiles)
    return pl.pallas_call(
        kern,
        out_shape=jax.ShapeDtypeStruct((N_TOK, D), F32),
        grid=(N_TOK // tm, n_f),
        in_specs=[
            pl.BlockSpec((tm, D), lambda i, f: (jnp.minimum(i, ctx_tiles - 1), 0)),
            pl.BlockSpec((tm, D), lambda i, f: (jnp.maximum(i - ctx_tiles, 0), 0)),
            pl.BlockSpec((tm, D), lambda i, f: (jnp.maximum(i - ctx_tiles, 0) % pos_tiles, 0),
                         pipeline_mode=pl.Buffered(1)),
        ] + _ffn_specs(tm, tf),
        out_specs=pl.BlockSpec((tm, D), lambda i, f: (i, 0)),
        scratch_shapes=[pltpu.VMEM((tm, D), BF16), pltpu.VMEM((tm, D), F32)],
        compiler_params=_cparams(("parallel", "arbitrary")),
        name="ffn_first",
    )(xc, xl, pos, modss, gain, w_in, w_in, w_out)


def _ffn_last_call(x, modss, gain, w_in, w_out, gain2, *, tm=1024, tf=512):
    n_f = D_FF // tf
    ctx_tiles = N_CTX // tm
    kern = functools.partial(_ffn_last_kernel, n_f=n_f, ctx_tiles=ctx_tiles)
    return pl.pallas_call(
        kern,
        out_shape=(jax.ShapeDtypeStruct((N_CTX, D), F32), jax.ShapeDtypeStruct((N_TOK - N_CTX, D), F32)),
        grid=(N_TOK // tm, n_f),
        in_specs=[pl.BlockSpec((tm, D), lambda i, f: (i, 0))] + _ffn_specs(tm, tf) + [
            pl.BlockSpec((1, D), lambda i, f: (0, 0))],
        out_specs=(pl.BlockSpec((tm, D), lambda i, f: (jnp.minimum(i, ctx_tiles - 1), 0)),
                   pl.BlockSpec((tm, D), lambda i, f: (jnp.maximum(i - ctx_tiles, 0), 0))),
        scratch_shapes=[pltpu.VMEM((tm, D), BF16), pltpu.VMEM((tm, D), F32)],
        compiler_params=_cparams(("parallel", "arbitrary")),
        name="ffn_last",
    )(x, modss, gain, w_in, w_in, w_out, gain2)


def _mixin_rwkv_kernel(x_ref, mod_ref, gain_ref, w_ref, mu_ref, keep_prev_ref, keep_next_ref, o_ref, h_scr):
    j = pl.program_id(1)

    @pl.when(j == 0)
    def _():
        y = _rms(x_ref[...], gain_ref[...])
        h_scr[...] = (y * (1.0 + mod_ref[4:5, :]) + mod_ref[3:4, :]).astype(BF16)

    u = _dot(h_scr[...], w_ref[...])
    rows, cols = u.shape
    keep_prev = keep_prev_ref[...]
    keep_next = keep_next_ref[...]
    lanes = keep_prev.shape[1]
    for l in range(cols // lanes):
        ls = slice(l * lanes, (l + 1) * lanes)
        ul = u[:, ls]
        neighbours = pltpu.roll(ul, 1, 0) * keep_prev + pltpu.roll(ul, rows - 1, 0) * keep_next
        mu = mu_ref[:, ls]
        o_ref[:, ls] = ul * (1.0 - mu) + neighbours * (0.5 * mu)


def _mixin_rwkv_call(x, modss, gain, w, mu, *, tn=512, lanes=128):
    tm = SS_LEN
    pos = lax.broadcasted_iota(jnp.int32, (2, tm, lanes), 1)
    seq_len = jnp.array([CTX_T, LAT_T], jnp.int32)[:, None, None]
    pos = pos & (seq_len - 1)
    keep_prev = jnp.where(pos == 0, 0.0, 1.0)
    keep_next = jnp.where(pos == seq_len - 1, 0.0, 1.0)
    kind = lambda i, j: (jnp.where(i < N_CTX_SS, 0, 1), 0, 0)
    return pl.pallas_call(
        _mixin_rwkv_kernel,
        out_shape=jax.ShapeDtypeStruct((N_TOK, N_RWKV_PAD), F32),
        grid=(N_TOK // tm, N_RWKV_PAD // tn),
        in_specs=[
            pl.BlockSpec((tm, D), lambda i, j: (i, 0)),
            pl.BlockSpec((None, N_MOD, D), lambda i, j: (i, 0, 0)),
            pl.BlockSpec((1, D), lambda i, j: (0, 0)),
            pl.BlockSpec((D, tn), lambda i, j: (0, j)),
            pl.BlockSpec((1, tn), lambda i, j: (0, j)),
            pl.BlockSpec((None, tm, lanes), kind),
            pl.BlockSpec((None, tm, lanes), kind),
        ],
        out_specs=pl.BlockSpec((tm, tn), lambda i, j: (i, j)),
        scratch_shapes=[pltpu.VMEM((tm, D), BF16)],
        compiler_params=_cparams(("parallel", "arbitrary")),
        name="mixin_rwkv",
    )(x, modss, gain, w, mu, keep_prev, keep_next)


def _mixin_gate_kernel(x_ref, mod_ref, gain_ref, wa_ref, wb_ref, wga_ref, wgb_ref, z_ref, ga_ref, gb_ref,
                       h_scr):
    j = pl.program_id(1)

    @pl.when(j == 0)
    def _():
        y = _rms(x_ref[...], gain_ref[...])
        h_scr[...] = (y * (1.0 + mod_ref[4:5, :]) + mod_ref[3:4, :]).astype(BF16)

    h = h_scr[...]
    z_ref[...] = _dot(h, wa_ref[...]) * _sigmoid(_dot(h, wb_ref[...]))
    ga_ref[...] = _sigmoid(_dot(h, wga_ref[...]))
    gb_ref[...] = _sigmoid(_dot(h, wgb_ref[...]))


def _mixin_gate_call(x, modss, gain, w_glu, w_gate, *, tm=1024, tn=512):
    n_j = D // tn
    ss_per = SS_LEN // tm
    sds = jax.ShapeDtypeStruct((N_TOK, D), F32)
    lo = lambda i, j: (0, j)
    hi = lambda i, j: (0, n_j + j)
    out = pl.BlockSpec((tm, tn), lambda i, j: (i, j))
    return pl.pallas_call(
        _mixin_gate_kernel,
        out_shape=(sds, sds, sds),
        grid=(N_TOK // tm, n_j),
        in_specs=[
            pl.BlockSpec((tm, D), lambda i, j: (i, 0)),
            pl.BlockSpec((None, N_MOD, D), lambda i, j: (i // ss_per, 0, 0)),
            pl.BlockSpec((1, D), lambda i, j: (0, 0)),
            pl.BlockSpec((D, tn), lo),
            pl.BlockSpec((D, tn), hi),
            pl.BlockSpec((D, tn), lo),
            pl.BlockSpec((D, tn), hi),
        ],
        out_specs=(out, out, out),
        scratch_shapes=[pltpu.VMEM((tm, D), BF16)],
        compiler_params=_cparams(("parallel", "arbitrary")),
        name="mixin_gate",
    )(x, modss, gain, w_glu, w_glu, w_gate, w_gate)


def _scan_prep(x, d, reverse, prm):
    (w0_ref, wup_ref, a0_ref, aup_ref, kk_ref, ka_ref, rk_ref, bd_ref) = prm
    c = CHUNK
    r = x[:, 0:D]
    k = x[:, D:2 * D]
    v = x[:, 2 * D:3 * D]
    wd = x[:, 3 * D + d * LORA_W:3 * D + (d + 1) * LORA_W]
    ad_off = 3 * D + 2 * LORA_W
    ad = x[:, ad_off + d * LORA_A:ad_off + (d + 1) * LORA_A]
    bd = bd_ref[...]

    logw = -DECAY_SCALE * _sigmoid(w0_ref[d] + _dot(jnp.tanh(wd).astype(BF16), wup_ref[d]))
    a = _sigmoid(a0_ref[d] + _dot(ad.astype(BF16), aup_ref[d]))
    kk = k * kk_ref[...]
    kk = kk * lax.rsqrt(_head_stat(kk * kk, bd) + 1e-12)
    kd = k * (1.0 + (a - 1.0) * ka_ref[...])
    b = kk * a
    bonus = _head_stat(r * kd * rk_ref[...], bd) * v

    ti = lax.broadcasted_iota(jnp.int32, (c, c), 0)
    si = lax.broadcasted_iota(jnp.int32, (c, c), 1)
    tri = jnp.where((ti <= si) if reverse else (ti >= si), 1.0, 0.0).astype(BF16)
    lw_hi, lw_lo = _split_bf16(logw)
    cum = _dot(tri, lw_hi) + _dot(tri, lw_lo)
    tot = cum[0:1, :] if reverse else cum[c - 1:c, :]
    p_inv = jnp.exp(-cum)
    p_tot = jnp.exp(tot)
    r_b = b * p_inv
    r_k = kd * p_inv
    return dict(
        l_kk=(kk * jnp.exp(cum - logw)).astype(BF16), l_r=(r * jnp.exp(cum)).astype(BF16),
        r_b=r_b, r_k=r_k, s_b=(r_b * p_tot).astype(BF16), s_k=(r_k * p_tot).astype(BF16),
        v=v, p_tot=p_tot, bonus=bonus)


def _group_norm(y, bd, lnw, lnb):
    yc = y - _head_stat(y, bd) * (1.0 / HEAD_DIM)
    var = _head_stat(yc * yc, bd) * (1.0 / HEAD_DIM)
    return yc * lax.rsqrt(var + GN_EPS) * lnw + lnb


def _scan_kernel(xf_ref, xb_ref, s0f_ref, s0b_ref, w0_ref, wup_ref, a0_ref, aup_ref, kk_ref, ka_ref,
                 rk_ref, bd_ref, lnw_ref, lnb_ref, y_ref, sf_ref, sb_ref, s_scr, bonus_scr):
    s = pl.program_id(0)
    is_ctx = s < CTX_STEPS
    n_chunks = jnp.where(is_ctx, CTX_CHUNKS, LAT_CHUNKS)
    t_step = jnp.where(is_ctx, s, s - CTX_STEPS)
    cidx = t_step & (n_chunks - 1)
    seq_in_ss = jnp.where(is_ctx, (t_step >> (CTX_CHUNKS.bit_length() - 1)) & (SS_LEN // CTX_T - 1), 0)
    row_f = pl.multiple_of((seq_in_ss * n_chunks + cidx) * CHUNK, CHUNK)
    row_b = pl.multiple_of((seq_in_ss * n_chunks + n_chunks - 1 - cidx) * CHUNK, CHUNK)
    second_visit = cidx >= (n_chunks >> 1)
    prm = (w0_ref, wup_ref, a0_ref, aup_ref, kk_ref, ka_ref, rk_ref, bd_ref)
    c, g, gw = CHUNK, HEAD_GROUP, HEAD_GROUP * HEAD_DIM
    n_groups = N_HEADS // g
    s0_refs = (s0f_ref, s0b_ref)

    @pl.when(jnp.logical_and(cidx == 0, is_ctx))
    def _():
        s_scr[...] = jnp.zeros_like(s_scr)

    @pl.when(jnp.logical_and(cidx == 0, jnp.logical_not(is_ctx)))
    def _():
        for d in range(2):
            for gi in range(n_groups):
                for l in range(g):
                    pieces = [jnp.zeros((HEAD_DIM, HEAD_DIM), F32)] * g
                    pieces[l] = s0_refs[d][gi * g + l]
                    s_scr[d, gi, l * HEAD_DIM:(l + 1) * HEAD_DIM, :] = jnp.concatenate(pieces, axis=1)

    pre = (_scan_prep(xf_ref[...], 0, False, prm), _scan_prep(xb_ref[...], 1, True, prm))

    row = lax.broadcasted_iota(jnp.int32, (c, gw), 0)
    lane = lax.broadcasted_iota(jnp.int32, (c, gw), 1)
    col = lane & (HEAD_DIM - 1)
    head_of_lane = lane >> 6
    eye = jnp.where(row == col, 1.0, 0.0)
    strict = (row > col, row < col)
    incl = (row >= col, row <= col)
    srow = lax.broadcasted_iota(jnp.int32, (gw, gw), 0) >> 6
    slane = lax.broadcasted_iota(jnp.int32, (gw, gw), 1) >> 6
    state_mask = srow == slane
    nt = (((1,), (1,)), ((), ()))
    tn = (((0,), (0,)), ((), ()))

    def blockdiag(x):
        return jnp.concatenate([jnp.where(head_of_lane == l, x, 0.0) for l in range(g)], axis=0).astype(BF16)

    def same_block(n):
        shift = n.bit_length() - 1
        return (row >> shift) == (col >> shift)

    items = [(d, gi) for d in range(2) for gi in range(n_groups)]
    cols = lambda gi: slice(gi * gw, (gi + 1) * gw)

    s_old = [s_scr[d, gi] for d, gi in items]
    lhs = [jnp.concatenate([pre[d]["l_kk"][:, cols(gi)], pre[d]["l_r"][:, cols(gi)]], axis=0) for d, gi in items]
    rt = [jnp.concatenate([blockdiag(pre[d]["r_b"][:, cols(gi)]), blockdiag(pre[d]["r_k"][:, cols(gi)])], axis=0)
          for d, gi in items]
    amat = [lax.dot_general(lhs[i], rt[i], nt, preferred_element_type=F32) for i in range(len(items))]
    ls = [lax.dot_general(lhs[i], s_old[i].astype(BF16), nt, preferred_element_type=F32) for i in range(len(items))]
    a_ab = [jnp.where(strict[d], amat[i][:c, :gw], 0.0) for i, (d, gi) in enumerate(items)]
    a_ak = [jnp.where(strict[d], amat[i][:c, gw:], 0.0).astype(BF16) for i, (d, gi) in enumerate(items)]
    a_r = [jnp.concatenate([jnp.where(incl[d], amat[i][c:, :gw], 0.0), jnp.where(incl[d], amat[i][c:, gw:], 0.0)],
                           axis=1).astype(BF16) for i, (d, gi) in enumerate(items)]
    bd_v = [blockdiag(pre[d]["v"][:, cols(gi)]) for d, gi in items]
    rhs = [-(ls[i][:c] + _dot(a_ak[i], bd_v[i])) for i in range(len(items))]

    tinv = [eye - jnp.where(same_block(2), a, 0.0) for a in a_ab]
    n = 2
    while n < c:
        lower = jnp.logical_and(same_block(2 * n), jnp.logical_not(same_block(n)))
        off = [jnp.where(lower, a, 0.0).astype(BF16) for a in a_ab]
        w = [_dot(off[i], blockdiag(tinv[i])) for i in range(len(items))]
        tinv = [tinv[i] - _dot(tinv[i].astype(BF16), blockdiag(w[i])) for i in range(len(items))]
        n *= 2

    u = [_dot(tinv[i].astype(BF16), blockdiag(rhs[i])) for i in range(len(items))]
    y = [ls[i][c:] + _dot(a_r[i], jnp.concatenate([blockdiag(u[i]), bd_v[i]], axis=0)) for i in range(len(items))]
    for i, (d, gi) in enumerate(items):
        uv = jnp.concatenate([u[i].astype(BF16), pre[d]["v"][:, cols(gi)].astype(BF16)], axis=0)
        sbk = jnp.concatenate([pre[d]["s_b"][:, cols(gi)], pre[d]["s_k"][:, cols(gi)]], axis=0)
        upd = lax.dot_general(uv, sbk, tn, preferred_element_type=F32)
        s_scr[d, gi] = s_old[i] * pre[d]["p_tot"][:, cols(gi)] + jnp.where(state_mask, upd, 0.0)

    visits = ((row_f, jnp.concatenate(y[:n_groups], axis=1), pre[0]["bonus"]),
              (row_b, jnp.concatenate(y[n_groups:], axis=1), pre[1]["bonus"]))

    @pl.when(jnp.logical_not(second_visit))
    def _():
        for rows, y_d, bonus_d in visits:
            y_ref[pl.ds(rows, c), :] = y_d
            bonus_scr[pl.ds(rows, c), :] = bonus_d

    @pl.when(second_visit)
    def _():
        for rows, y_d, bonus_d in visits:
            y_sum = y_ref[pl.ds(rows, c), :] + y_d
            y_ref[pl.ds(rows, c), :] = (_group_norm(y_sum, bd_ref[...], lnw_ref[...], lnb_ref[...])
                                         + bonus_scr[pl.ds(rows, c), :] + bonus_d)

    @pl.when(jnp.logical_and(cidx == n_chunks - 1, is_ctx))
    def _():
        for d, out_ref in enumerate((sf_ref, sb_ref)):
            for h in range(N_HEADS):
                gi, l = divmod(h, g)
                hs = slice(l * HEAD_DIM, (l + 1) * HEAD_DIM)
                out_ref[h] = s_scr[d, gi, hs, hs]


def _scan_row_block(s, reverse):
    ctx = s < CTX_STEPS
    n_chunks = jnp.where(ctx, CTX_CHUNKS, LAT_CHUNKS)
    t = jnp.where(ctx, s, s - CTX_STEPS)
    cidx = t & (n_chunks - 1)
    if reverse:
        cidx = n_chunks - 1 - cidx
    base = jnp.where(ctx, 0, CTX_STEPS)
    return base + (t - (t & (n_chunks - 1))) + cidx


def _scan_call(u, s0f, s0b, w0, wup, a0, aup, k_k, k_a, r_k, bd, lnw, lnb):
    assert CTX_STEPS // N_CTX_SS == LAT_CHUNKS, "context and latent super sequences take the same number of steps"
    steps_per_ss = LAT_CHUNKS
    fwd = lambda s: (_scan_row_block(s, False), 0)
    bwd = lambda s: (_scan_row_block(s, True), 0)
    lat_seq = lambda s: (jnp.maximum(s - CTX_STEPS, 0) // LAT_CHUNKS, 0, 0, 0)
    ctx_seq = lambda s: (jnp.minimum(s // CTX_CHUNKS, CTX_B - 1), 0, 0, 0)
    const2 = lambda s: (0, 0)
    const3 = lambda s: (0, 0, 0)
    st_block = (None, N_HEADS, HEAD_DIM, HEAD_DIM)
    y_sds = jax.ShapeDtypeStruct((N_TOK, D), F32)
    s_sds = jax.ShapeDtypeStruct((CTX_B, N_HEADS, HEAD_DIM, HEAD_DIM), F32)
    return pl.pallas_call(
        _scan_kernel,
        out_shape=(y_sds, s_sds, s_sds),
        grid=(CTX_STEPS + LAT_STEPS,),
        in_specs=[
            pl.BlockSpec((CHUNK, N_RWKV_COLS), fwd),
            pl.BlockSpec((CHUNK, N_RWKV_COLS), bwd),
            pl.BlockSpec(st_block, lat_seq),
            pl.BlockSpec(st_block, lat_seq),
            pl.BlockSpec((2, 1, D), const3),
            pl.BlockSpec((2, LORA_W, D), const3),
            pl.BlockSpec((2, 1, D), const3),
            pl.BlockSpec((2, LORA_A, D), const3),
            pl.BlockSpec((1, D), const2),
            pl.BlockSpec((1, D), const2),
            pl.BlockSpec((1, D), const2),
            pl.BlockSpec((STAT_W, STAT_W), const2),
            pl.BlockSpec((1, D), const2),
            pl.BlockSpec((1, D), const2),
        ],
        out_specs=(
            pl.BlockSpec((SS_LEN, D), lambda s: (s // steps_per_ss, 0)),
            pl.BlockSpec(st_block, ctx_seq),
            pl.BlockSpec(st_block, ctx_seq),
        ),
        scratch_shapes=[
            pltpu.VMEM((2, N_HEADS // HEAD_GROUP, HEAD_GROUP * HEAD_DIM, HEAD_GROUP * HEAD_DIM), F32),
            pltpu.VMEM((SS_LEN, D), F32),
        ],
        compiler_params=_cparams(("arbitrary",)),
        name="wkv7_scan",
    )(u, u, s0f, s0b, w0, wup, a0, aup, k_k, k_a, r_k, bd, lnw, lnb)


def _conv_kernel(z_ref, zp_ref, zn_ref, w_ref, b_ref, lnw_ref, lnb_ref, o_ref, pad_scr, *, tm, halo):
    i = pl.program_id(0)
    tiles_per_lat = LAT_T // tm
    j = (i - N_CTX // tm) % tiles_per_lat
    is_ctx = i < N_CTX // tm
    keep_prev = jnp.where(jnp.logical_or(is_ctx, j == 0), 0.0, 1.0)
    keep_next = jnp.where(jnp.logical_or(is_ctx, j == tiles_per_lat - 1), 0.0, 1.0)
    pad_scr[0, 0:halo, :] = zp_ref[...] * keep_prev
    pad_scr[0, halo:halo + tm, :] = z_ref[...]
    pad_scr[0, halo + tm:halo + tm + halo, :] = zn_ref[...] * keep_next
    span = tm + 2 * halo - SUBLANES
    for m in range(1, SUBLANES):
        pad_scr[m, 0:span, :] = pad_scr[0, m:m + span, :]

    rb = 128
    lanes = 128

    def lane_block(cb, carry):
        cs = pl.ds(pl.multiple_of(cb * lanes, lanes), lanes)
        for r0 in range(0, tm, rb):
            acc = jnp.broadcast_to(b_ref[:, cs], (rb, lanes))
            for tap in range(CONV_WIDTH):
                q, m = divmod(tap + halo - CONV_HALF, SUBLANES)
                start = r0 + q * SUBLANES
                acc = acc + w_ref[tap:tap + 1, cs] * pad_scr[m, start:start + rb, cs]
            o_ref[r0:r0 + rb, cs] = acc
        return carry

    lax.fori_loop(0, D // lanes, lane_block, 0)
    y = o_ref[...]
    mean = jnp.mean(y, axis=-1, keepdims=True)
    yc = y - mean
    var = jnp.mean(yc * yc, axis=-1, keepdims=True)
    y = yc * lax.rsqrt(var + LN_EPS) * lnw_ref[...] + lnb_ref[...]
    o_ref[...] = y * _sigmoid(y)


def _conv_call(z, w, b, lnw, lnb, *, tm=256, halo=16):
    hb = tm // halo
    n_h = N_TOK // halo
    kern = functools.partial(_conv_kernel, tm=tm, halo=halo)
    return pl.pallas_call(
        kern,
        out_shape=jax.ShapeDtypeStruct((N_TOK, D), F32),
        grid=(N_TOK // tm,),
        in_specs=[
            pl.BlockSpec((tm, D), lambda i: (i, 0)),
            pl.BlockSpec((halo, D), lambda i: (jnp.maximum(i * hb - 1, 0), 0)),
            pl.BlockSpec((halo, D), lambda i: (jnp.minimum((i + 1) * hb, n_h - 1), 0)),
            pl.BlockSpec((CONV_WIDTH, D), lambda i: (0, 0)),
            pl.BlockSpec((1, D), lambda i: (0, 0)),
            pl.BlockSpec((1, D), lambda i: (0, 0)),
            pl.BlockSpec((1, D), lambda i: (0, 0)),
        ],
        out_specs=pl.BlockSpec((tm, D), lambda i: (i, 0)),
        scratch_shapes=[pltpu.VMEM((SUBLANES, tm + 2 * halo, D), F32)],
        compiler_params=_cparams(("parallel",)),
        name="conv_branch",
    )(z, z, z, w, b, lnw, lnb)


def _mixout_kernel(x_ref, mod_ref, y_ref, gd_ref, ga_ref, gb_ref, yc_ref, gup_ref, wo_ref, o_ref):
    g = _dot(_sigmoid(gd_ref[...]).astype(BF16), gup_ref[...])
    mixed = ga_ref[...] * (y_ref[...] * g) + gb_ref[...] * yc_ref[...]
    o_ref[...] = x_ref[...] + mod_ref[5:6, :] * _dot(mixed.astype(BF16), wo_ref[...])


def _mixout_call(x, modss, y_att, u, ga, gb, yconv, g_up, w_out, *, tm=512):
    ss_per = SS_LEN // tm
    tok = lambda i: (i, 0)
    const2 = lambda i: (0, 0)
    gd_blk = (3 * D + 2 * LORA_W + 2 * LORA_A) // LORA_G
    return pl.pallas_call(
        _mixout_kernel,
        out_shape=jax.ShapeDtypeStruct((N_TOK, D), F32),
        grid=(N_TOK // tm,),
        in_specs=[
            pl.BlockSpec((tm, D), tok),
            pl.BlockSpec((None, N_MOD, D), lambda i: (i // ss_per, 0, 0)),
            pl.BlockSpec((tm, D), tok),
            pl.BlockSpec((tm, LORA_G), lambda i: (i, gd_blk)),
            pl.BlockSpec((tm, D), tok),
            pl.BlockSpec((tm, D), tok),
            pl.BlockSpec((tm, D), tok),
            pl.BlockSpec((LORA_G, D), const2),
            pl.BlockSpec((D, D), const2),
        ],
        out_specs=pl.BlockSpec((tm, D), tok),
        compiler_params=_cparams(("parallel",)),
        name="mixout",
    )(x, modss, y_att, u, ga, gb, yconv, g_up, w_out)


def _grid_pos_embed(rows):
    row, col = jnp.meshgrid(jnp.arange(rows, dtype=F32), jnp.arange(GRID_W, dtype=F32), indexing="ij")
    n_freq = D // 4
    freqs = jnp.exp(-math.log(POS_MAX_PERIOD) * jnp.arange(n_freq, dtype=F32) / n_freq)
    ang_r = row.reshape(-1, 1) * freqs
    ang_c = col.reshape(-1, 1) * freqs
    return jnp.concatenate([jnp.sin(ang_r), jnp.cos(ang_r), jnp.sin(ang_c), jnp.cos(ang_c)], axis=-1)


def kernel(x_prompt, x_sample, state_rwkv_fwd, state_rwkv_bwd, c, c_ctx, w_mod, b_mod, norm_ffn1, ffn1_w_in, ffn1_w_out, norm_mix, mix_w_in, rwkv_mu, rwkv_w0, rwkv_w_up, rwkv_a0, rwkv_a_up, rwkv_g_up, rwkv_k_k, rwkv_k_a, rwkv_r_k, rwkv_lnx_w, rwkv_lnx_b, conv_dw, conv_db, conv_ln_w, conv_ln_b, mix_w_out, norm_ffn2, ffn2_w_in, ffn2_w_out, norm_final):
    assert x_prompt.shape == (CTX_B, CTX_T, D) and x_sample.shape == (LAT_B, LAT_T, D)
    assert w_mod.shape[0] == 1, "single-layer trunk"
    row = lambda t: t.reshape(1, -1)

    cond = jnp.concatenate([c, c_ctx[None], jnp.zeros((8 - LAT_B - 1, D), F32)], axis=0)
    mod = _mod_call(cond, w_mod[0], row(b_mod[0])).reshape(8, N_MOD, D)
    ss_cond = jnp.array([LAT_B] * N_CTX_SS + list(range(LAT_B)), jnp.int32)
    modss = mod[ss_cond]

    pos = _grid_pos_embed(LAT_T // GRID_W).astype(x_sample.dtype)

    w_mix = mix_w_in[0]
    glu_off = N_RWKV_COLS
    gate_off = N_RWKV_COLS + 2 * D
    col_pad = ((0, 0), (0, N_RWKV_PAD - N_RWKV_COLS))
    w_rwkv = jnp.pad(w_mix[:, :glu_off], col_pad).astype(BF16)
    w_glu = w_mix[:, glu_off:gate_off].astype(BF16)
    w_gate = w_mix[:, gate_off:].astype(BF16)

    x1 = _ffn_first_call(x_prompt.reshape(N_CTX, D), x_sample.reshape(LAT_B * LAT_T, D), pos, modss,
                         row(norm_ffn1[0]), ffn1_w_in[0], ffn1_w_out[0])

    u = _mixin_rwkv_call(x1, modss, row(norm_mix[0]), w_rwkv, jnp.pad(row(rwkv_mu[0]), col_pad))
    z, ga, gb = _mixin_gate_call(x1, modss, row(norm_mix[0]), w_glu, w_gate)

    ii = lax.broadcasted_iota(jnp.int32, (STAT_W, STAT_W), 0) // HEAD_DIM
    jj = lax.broadcasted_iota(jnp.int32, (STAT_W, STAT_W), 1) // HEAD_DIM
    bd = jnp.where(ii == jj, 1.0, 0.0).astype(BF16)

    y_att, s_f, s_b = _scan_call(
        u, state_rwkv_fwd[:, 0], state_rwkv_bwd[:, 0],
        rwkv_w0[0].reshape(2, 1, D), rwkv_w_up[0].astype(BF16), rwkv_a0[0].reshape(2, 1, D),
        rwkv_a_up[0].astype(BF16), row(rwkv_k_k[0]), row(rwkv_k_a[0]), row(rwkv_r_k[0]), bd,
        row(rwkv_lnx_w[0]), row(rwkv_lnx_b[0]))

    yconv = _conv_call(z, conv_dw[0], row(conv_db[0]), row(conv_ln_w[0]), row(conv_ln_b[0]))

    x2 = _mixout_call(x1, modss, y_att, u, ga, gb, yconv, rwkv_g_up[0].astype(BF16),
                      mix_w_out[0].astype(BF16))

    y_ctx, y_lat = _ffn_last_call(x2, modss, row(norm_ffn2[0]), ffn2_w_in[0], ffn2_w_out[0], row(norm_final))

    st = (CTX_B, 1, N_HEADS, HEAD_DIM, HEAD_DIM)
    return (y_ctx.reshape(CTX_B, CTX_T, D), y_lat.reshape(LAT_B, LAT_T, D),
            s_f.reshape(st).astype(x_prompt.dtype), s_b.reshape(st).astype(x_prompt.dtype))
```

```python
import functools
import math

import jax
import jax.numpy as jnp
from jax import lax
from jax.experimental import pallas as pl
from jax.experimental.pallas import tpu as pltpu

F32 = jnp.float32
BF16 = jnp.bfloat16

D = 1024
N_HEADS = 16
HEAD_DIM = 64
D_FF = 4 * D
N_MOD = 9
CONV_WIDTH = 31
CONV_HALF = CONV_WIDTH // 2
LORA_W = 64
LORA_A = 64
LORA_G = 128
N_RWKV_COLS = 3 * D + 2 * LORA_W + 2 * LORA_A + LORA_G
N_RWKV_PAD = 3584
GRID_W = 64
EPS = 1e-6
LN_EPS = 1e-5
GN_EPS = 64e-5
DECAY_SCALE = 0.606531
POS_MAX_PERIOD = 10000.0

CTX_B, CTX_T = 32, 256
LAT_B, LAT_T = 4, 2048
N_CTX = CTX_B * CTX_T
N_TOK = N_CTX + LAT_B * LAT_T
SS_LEN = 2048
N_SS = N_TOK // SS_LEN
N_CTX_SS = N_CTX // SS_LEN

CHUNK = 64
HEAD_GROUP = 2
CTX_CHUNKS = CTX_T // CHUNK
LAT_CHUNKS = LAT_T // CHUNK
CTX_STEPS = CTX_B * CTX_CHUNKS
LAT_STEPS = LAT_B * LAT_CHUNKS
STAT_W = 256
SUBLANES = 8

VMEM_LIMIT = 58 * 1024 * 1024


def _cparams(sem):
    return pltpu.CompilerParams(dimension_semantics=sem, vmem_limit_bytes=VMEM_LIMIT)


def _sigmoid(x):
    return 1.0 / (1.0 + jnp.exp(-x))


def _dot(a, b):
    return jnp.dot(a, b, preferred_element_type=F32)


def _split_bf16(x):
    hi = x.astype(BF16)
    lo = (x - hi.astype(F32)).astype(BF16)
    return hi, lo


def _head_stat(x, bd):
    groups = D // STAT_W
    stacked = jnp.concatenate([x[:, g * STAT_W:(g + 1) * STAT_W] for g in range(groups)], axis=0)
    s = _dot(stacked.astype(BF16), bd)
    rows = x.shape[0]
    return jnp.concatenate([s[g * rows:(g + 1) * rows] for g in range(groups)], axis=1)


def _mod_kernel(cond_ref, w_ref, b_ref, o_ref):
    c = cond_ref[...]
    s = (c * _sigmoid(c)).astype(BF16)
    o_ref[...] = _dot(s, w_ref[...].astype(BF16)) + b_ref[...]


def _mod_call(cond, w_mod, b_mod):
    tn = 1152
    n = N_MOD * D
    return pl.pallas_call(
        _mod_kernel,
        out_shape=jax.ShapeDtypeStruct((8, n), F32),
        grid=(n // tn,),
        in_specs=[
            pl.BlockSpec((8, D), lambda j: (0, 0)),
            pl.BlockSpec((D, tn), lambda j: (0, j)),
            pl.BlockSpec((1, tn), lambda j: (0, j)),
        ],
        out_specs=pl.BlockSpec((8, tn), lambda j: (0, j)),
        compiler_params=_cparams(("arbitrary",)),
        name="mod",
    )(cond, w_mod, b_mod)


def _rms(x, gain):
    return x * lax.rsqrt(jnp.mean(x * x, axis=-1, keepdims=True) + EPS) * gain


def _ffn_start(x, mod_ref, gain_ref, h_scr, acc_scr, mod_row):
    y = _rms(x, gain_ref[...])
    h = y * (1.0 + mod_ref[mod_row + 1:mod_row + 2, :]) + mod_ref[mod_row:mod_row + 1, :]
    h_scr[...] = h.astype(BF16)
    acc_scr[...] = jnp.zeros_like(acc_scr)


def _ffn_step(wg_ref, wu_ref, wo_ref, h_scr, acc_scr):
    h = h_scr[...]
    g = _dot(h, wg_ref[...].astype(BF16))
    u = _dot(h, wu_ref[...].astype(BF16))
    a = (g * _sigmoid(g) * u).astype(BF16)
    acc_scr[...] += _dot(a, wo_ref[...].astype(BF16))


def _ffn_first_kernel(xc_ref, xl_ref, pos_ref, mod_ref, gain_ref, wg_ref, wu_ref, wo_ref, o_ref,
                      h_scr, acc_scr, *, n_f, ctx_tiles):
    i = pl.program_id(0)
    f = pl.program_id(1)
    is_ctx = i < ctx_tiles
    sources = ((is_ctx, lambda: xc_ref[...]), (jnp.logical_not(is_ctx), lambda: xl_ref[...] + pos_ref[...]))

    for cond, x_in in sources:
        @pl.when(jnp.logical_and(f == 0, cond))
        def _():
            _ffn_start(x_in(), mod_ref, gain_ref, h_scr, acc_scr, 0)

    _ffn_step(wg_ref, wu_ref, wo_ref, h_scr, acc_scr)

    for cond, x_in in sources:
        @pl.when(jnp.logical_and(f == n_f - 1, cond))
        def _():
            o_ref[...] = x_in() + mod_ref[2:3, :] * (0.5 * acc_scr[...])


def _ffn_last_kernel(x_ref, mod_ref, gain_ref, wg_ref, wu_ref, wo_ref, gain2_ref, oc_ref, ol_ref,
                     h_scr, acc_scr, *, n_f, ctx_tiles):
    i = pl.program_id(0)
    f = pl.program_id(1)

    @pl.when(f == 0)
    def _():
        _ffn_start(x_ref[...], mod_ref, gain_ref, h_scr, acc_scr, 6)

    _ffn_step(wg_ref, wu_ref, wo_ref, h_scr, acc_scr)

    def result():
        return _rms(x_ref[...] + mod_ref[8:9, :] * (0.5 * acc_scr[...]), gain2_ref[...])

    @pl.when(jnp.logical_and(f == n_f - 1, i < ctx_tiles))
    def _():
        oc_ref[...] = result()

    @pl.when(jnp.logical_and(f == n_f - 1, i >= ctx_tiles))
    def _():
        ol_ref[...] = result()


def _ffn_specs(tm, tf):
    n_f = D_FF // tf
    ss_per = SS_LEN // tm
    return [
        pl.BlockSpec((None, N_MOD, D), lambda i, f: (i // ss_per, 0, 0)),
        pl.BlockSpec((1, D), lambda i, f: (0, 0)),
        pl.BlockSpec((D, tf), lambda i, f: (0, f)),
        pl.BlockSpec((D, tf), lambda i, f: (0, n_f + f)),
        pl.BlockSpec((tf, D), lambda i, f: (f, 0)),
    ]


def _ffn_first_call(xc, xl, pos, modss, gain, w_in, w_out, *, tm=1024, tf=512):
    n_f = D_FF // tf
    ctx_tiles = N_CTX // tm
    pos_tiles = LAT_T // tm
    kern = functools.partial(_ffn_first_kernel, n_f=n_f, ctx_tiles=ctx_tiles)
    return pl.pallas_call(
        kern,
        out_shape=jax.ShapeDtypeStruct((N_TOK, D), F32),
        grid=(N_TOK // tm, n_f),
        in_specs=[
            pl.BlockSpec((tm, D), lambda i, f: (jnp.minimum(i, ctx_tiles - 1), 0)),
            pl.BlockSpec((tm, D), lambda i, f: (jnp.maximum(i - ctx_tiles, 0), 0)),
            pl.BlockSpec((tm, D), lambda i, f: (jnp.maximum(i - ctx_tiles, 0) % pos_tiles, 0)),
        ] + _ffn_specs(tm, tf),
        out_specs=pl.BlockSpec((tm, D), lambda i, f: (i, 0)),
        scratch_shapes=[pltpu.VMEM((tm, D), BF16), pltpu.VMEM((tm, D), F32)],
        compiler_params=_cparams(("parallel", "arbitrary")),
        name="ffn_first",
    )(xc, xl, pos, modss, gain, w_in, w_in, w_out)


def _ffn_last_call(x, modss, gain, w_in, w_out, gain2, *, tm=1024, tf=512):
    n_f = D_FF // tf
    ctx_tiles = N_CTX // tm
    kern = functools.partial(_ffn_last_kernel, n_f=n_f, ctx_tiles=ctx_tiles)
    return pl.pallas_call(
        kern,
        out_shape=(jax.ShapeDtypeStruct((N_CTX, D), F32), jax.ShapeDtypeStruct((N_TOK - N_CTX, D), F32)),
        grid=(N_TOK // tm, n_f),
        in_specs=[pl.BlockSpec((tm, D), lambda i, f: (i, 0))] + _ffn_specs(tm, tf) + [
            pl.BlockSpec((1, D), lambda i, f: (0, 0))],
        out_specs=(pl.BlockSpec((tm, D), lambda i, f: (jnp.minimum(i, ctx_tiles - 1), 0)),
                   pl.BlockSpec((tm, D), lambda i, f: (jnp.maximum(i - ctx_tiles, 0), 0))),
        scratch_shapes=[pltpu.VMEM((tm, D), BF16), pltpu.VMEM((tm, D), F32)],
        compiler_params=_cparams(("parallel", "arbitrary")),
        name="ffn_last",
    )(x, modss, gain, w_in, w_in, w_out, gain2)


def _mixin_rwkv_kernel(x_ref, mod_ref, gain_ref, w_ref, mu_ref, keep_prev_ref, keep_next_ref, o_ref, h_scr):
    j = pl.program_id(1)

    @pl.when(j == 0)
    def _():
        y = _rms(x_ref[...], gain_ref[...])
        h_scr[...] = (y * (1.0 + mod_ref[4:5, :]) + mod_ref[3:4, :]).astype(BF16)

    u = _dot(h_scr[...], w_ref[...])
    rows, cols = u.shape
    keep_prev = keep_prev_ref[...]
    keep_next = keep_next_ref[...]
    lanes = keep_prev.shape[1]
    for l in range(cols // lanes):
        ls = slice(l * lanes, (l + 1) * lanes)
        ul = u[:, ls]
        neighbours = pltpu.roll(ul, 1, 0) * keep_prev + pltpu.roll(ul, rows - 1, 0) * keep_next
        mu = mu_ref[:, ls]
        o_ref[:, ls] = ul * (1.0 - mu) + neighbours * (0.5 * mu)


def _mixin_rwkv_call(x, modss, gain, w, mu, *, tn=512, lanes=128):
    tm = SS_LEN
    pos = lax.broadcasted_iota(jnp.int32, (2, tm, lanes), 1)
    seq_len = jnp.array([CTX_T, LAT_T], jnp.int32)[:, None, None]
    pos = pos & (seq_len - 1)
    keep_prev = jnp.where(pos == 0, 0.0, 1.0)
    keep_next = jnp.where(pos == seq_len - 1, 0.0, 1.0)
    kind = lambda i, j: (jnp.where(i < N_CTX_SS, 0, 1), 0, 0)
    return pl.pallas_call(
        _mixin_rwkv_kernel,
        out_shape=jax.ShapeDtypeStruct((N_TOK, N_RWKV_PAD), F32),
        grid=(N_TOK // tm, N_RWKV_PAD // tn),
        in_specs=[
            pl.BlockSpec((tm, D), lambda i, j: (i, 0)),
            pl.BlockSpec((None, N_MOD, D), lambda i, j: (i, 0, 0)),
            pl.BlockSpec((1, D), lambda i, j: (0, 0)),
            pl.BlockSpec((D, tn), lambda i, j: (0, j)),
            pl.BlockSpec((1, tn), lambda i, j: (0, j)),
            pl.BlockSpec((None, tm, lanes), kind),
            pl.BlockSpec((None, tm, lanes), kind),
        ],
        out_specs=pl.BlockSpec((tm, tn), lambda i, j: (i, j)),
        scratch_shapes=[pltpu.VMEM((tm, D), BF16)],
        compiler_params=_cparams(("parallel", "arbitrary")),
        name="mixin_rwkv",
    )(x, modss, gain, w, mu, keep_prev, keep_next)


def _mixin_gate_kernel(x_ref, mod_ref, gain_ref, wa_ref, wb_ref, wga_ref, wgb_ref, z_ref, ga_ref, gb_ref,
                       h_scr):
    j = pl.program_id(1)

    @pl.when(j == 0)
    def _():
        y = _rms(x_ref[...], gain_ref[...])
        h_scr[...] = (y * (1.0 + mod_ref[4:5, :]) + mod_ref[3:4, :]).astype(BF16)

    h = h_scr[...]
    z_ref[...] = _dot(h, wa_ref[...]) * _sigmoid(_dot(h, wb_ref[...]))
    ga_ref[...] = _sigmoid(_dot(h, wga_ref[...]))
    gb_ref[...] = _sigmoid(_dot(h, wgb_ref[...]))


def _mixin_gate_call(x, modss, gain, w_glu, w_gate, *, tm=1024, tn=512):
    n_j = D // tn
    ss_per = SS_LEN // tm
    sds = jax.ShapeDtypeStruct((N_TOK, D), F32)
    lo = lambda i, j: (0, j)
    hi = lambda i, j: (0, n_j + j)
    out = pl.BlockSpec((tm, tn), lambda i, j: (i, j))
    return pl.pallas_call(
        _mixin_gate_kernel,
        out_shape=(sds, sds, sds),
        grid=(N_TOK // tm, n_j),
        in_specs=[
            pl.BlockSpec((tm, D), lambda i, j: (i, 0)),
            pl.BlockSpec((None, N_MOD, D), lambda i, j: (i // ss_per, 0, 0)),
            pl.BlockSpec((1, D), lambda i, j: (0, 0)),
            pl.BlockSpec((D, tn), lo),
            pl.BlockSpec((D, tn), hi),
            pl.BlockSpec((D, tn), lo),
            pl.BlockSpec((D, tn), hi),
        ],
        out_specs=(out, out, out),
        scratch_shapes=[pltpu.VMEM((tm, D), BF16)],
        compiler_params=_cparams(("parallel", "arbitrary")),
        name="mixin_gate",
    )(x, modss, gain, w_glu, w_glu, w_gate, w_gate)


def _scan_prep(x, d, reverse, prm):
    (w0_ref, wup_ref, a0_ref, aup_ref, kk_ref, ka_ref, rk_ref, bd_ref) = prm
    c = CHUNK
    r = x[:, 0:D]
    k = x[:, D:2 * D]
    v = x[:, 2 * D:3 * D]
    wd = x[:, 3 * D + d * LORA_W:3 * D + (d + 1) * LORA_W]
    ad_off = 3 * D + 2 * LORA_W
    ad = x[:, ad_off + d * LORA_A:ad_off + (d + 1) * LORA_A]
    bd = bd_ref[...]

    logw = -DECAY_SCALE * _sigmoid(w0_ref[d] + _dot(jnp.tanh(wd).astype(BF16), wup_ref[d]))
    a = _sigmoid(a0_ref[d] + _dot(ad.astype(BF16), aup_ref[d]))
    kk = k * kk_ref[...]
    kk = kk * lax.rsqrt(_head_stat(kk * kk, bd) + 1e-12)
    kd = k * (1.0 + (a - 1.0) * ka_ref[...])
    b = kk * a
    bonus = _head_stat(r * kd * rk_ref[...], bd) * v

    ti = lax.broadcasted_iota(jnp.int32, (c, c), 0)
    si = lax.broadcasted_iota(jnp.int32, (c, c), 1)
    tri = jnp.where((ti <= si) if reverse else (ti >= si), 1.0, 0.0).astype(BF16)
    lw_hi, lw_lo = _split_bf16(logw)
    cum = _dot(tri, lw_hi) + _dot(tri, lw_lo)
    tot = cum[0:1, :] if reverse else cum[c - 1:c, :]
    p_inv = jnp.exp(-cum)
    p_tot = jnp.exp(tot)
    r_b = b * p_inv
    r_k = kd * p_inv
    return dict(
        l_kk=(kk * jnp.exp(cum - logw)).astype(BF16), l_r=(r * jnp.exp(cum)).astype(BF16),
        r_b=r_b, r_k=r_k, s_b=(r_b * p_tot).astype(BF16), s_k=(r_k * p_tot).astype(BF16),
        v=v, p_tot=p_tot, bonus=bonus)


def _group_norm(y, bd, lnw, lnb):
    yc = y - _head_stat(y, bd) * (1.0 / HEAD_DIM)
    var = _head_stat(yc * yc, bd) * (1.0 / HEAD_DIM)
    return yc * lax.rsqrt(var + GN_EPS) * lnw + lnb


def _scan_kernel(xf_ref, xb_ref, s0f_ref, s0b_ref, w0_ref, wup_ref, a0_ref, aup_ref, kk_ref, ka_ref,
                 rk_ref, bd_ref, lnw_ref, lnb_ref, y_ref, sf_ref, sb_ref, s_scr, bonus_scr):
    s = pl.program_id(0)
    is_ctx = s < CTX_STEPS
    n_chunks = jnp.where(is_ctx, CTX_CHUNKS, LAT_CHUNKS)
    t_step = jnp.where(is_ctx, s, s - CTX_STEPS)
    cidx = t_step & (n_chunks - 1)
    seq_in_ss = jnp.where(is_ctx, (t_step >> (CTX_CHUNKS.bit_length() - 1)) & (SS_LEN // CTX_T - 1), 0)
    row_f = pl.multiple_of((seq_in_ss * n_chunks + cidx) * CHUNK, CHUNK)
    row_b = pl.multiple_of((seq_in_ss * n_chunks + n_chunks - 1 - cidx) * CHUNK, CHUNK)
    second_visit = cidx >= (n_chunks >> 1)
    prm = (w0_ref, wup_ref, a0_ref, aup_ref, kk_ref, ka_ref, rk_ref, bd_ref)
    c, g, gw = CHUNK, HEAD_GROUP, HEAD_GROUP * HEAD_DIM
    n_groups = N_HEADS // g
    s0_refs = (s0f_ref, s0b_ref)

    @pl.when(jnp.logical_and(cidx == 0, is_ctx))
    def _():
        s_scr[...] = jnp.zeros_like(s_scr)

    @pl.when(jnp.logical_and(cidx == 0, jnp.logical_not(is_ctx)))
    def _():
        for d in range(2):
            for gi in range(n_groups):
                for l in range(g):
                    pieces = [jnp.zeros((HEAD_DIM, HEAD_DIM), F32)] * g
                    pieces[l] = s0_refs[d][gi * g + l]
                    s_scr[d, gi, l * HEAD_DIM:(l + 1) * HEAD_DIM, :] = jnp.concatenate(pieces, axis=1)

    pre = (_scan_prep(xf_ref[...], 0, False, prm), _scan_prep(xb_ref[...], 1, True, prm))

    row = lax.broadcasted_iota(jnp.int32, (c, gw), 0)
    lane = lax.broadcasted_iota(jnp.int32, (c, gw), 1)
    col = lane & (HEAD_DIM - 1)
    head_of_lane = lane >> 6
    eye = jnp.where(row == col, 1.0, 0.0)
    strict = (row > col, row < col)
    incl = (row >= col, row <= col)
    srow = lax.broadcasted_iota(jnp.int32, (gw, gw), 0) >> 6
    slane = lax.broadcasted_iota(jnp.int32, (gw, gw), 1) >> 6
    state_mask = srow == slane
    nt = (((1,), (1,)), ((), ()))
    tn = (((0,), (0,)), ((), ()))

    def blockdiag(x):
        return jnp.concatenate([jnp.where(head_of_lane == l, x, 0.0) for l in range(g)], axis=0).astype(BF16)

    def same_block(n):
        shift = n.bit_length() - 1
        return (row >> shift) == (col >> shift)

    items = [(d, gi) for d in range(2) for gi in range(n_groups)]
    cols = lambda gi: slice(gi * gw, (gi + 1) * gw)

    s_old = [s_scr[d, gi] for d, gi in items]
    lhs = [jnp.concatenate([pre[d]["l_kk"][:, cols(gi)], pre[d]["l_r"][:, cols(gi)]], axis=0) for d, gi in items]
    rt = [jnp.concatenate([blockdiag(pre[d]["r_b"][:, cols(gi)]), blockdiag(pre[d]["r_k"][:, cols(gi)])], axis=0)
          for d, gi in items]
    amat = [lax.dot_general(lhs[i], rt[i], nt, preferred_element_type=F32) for i in range(len(items))]
    ls = [lax.dot_general(lhs[i], s_old[i].astype(BF16), nt, preferred_element_type=F32) for i in range(len(items))]
    a_ab = [jnp.where(strict[d], amat[i][:c, :gw], 0.0) for i, (d, gi) in enumerate(items)]
    a_ak = [jnp.where(strict[d], amat[i][:c, gw:], 0.0).astype(BF16) for i, (d, gi) in enumerate(items)]
    a_r = [jnp.concatenate([jnp.where(incl[d], amat[i][c:, :gw], 0.0), jnp.where(incl[d], amat[i][c:, gw:], 0.0)],
                           axis=1).astype(BF16) for i, (d, gi) in enumerate(items)]
    bd_v = [blockdiag(pre[d]["v"][:, cols(gi)]) for d, gi in items]
    rhs = [-(ls[i][:c] + _dot(a_ak[i], bd_v[i])) for i in range(len(items))]

    tinv = [eye - jnp.where(same_block(2), a, 0.0) for a in a_ab]
    n = 2
    while n < c:
        lower = jnp.logical_and(same_block(2 * n), jnp.logical_not(same_block(n)))
        off = [jnp.where(lower, a, 0.0).astype(BF16) for a in a_ab]
        w = [_dot(off[i], blockdiag(tinv[i])) for i in range(len(items))]
        tinv = [tinv[i] - _dot(tinv[i].astype(BF16), blockdiag(w[i])) for i in range(len(items))]
        n *= 2

    u = [_dot(tinv[i].astype(BF16), blockdiag(rhs[i])) for i in range(len(items))]
    y = [ls[i][c:] + _dot(a_r[i], jnp.concatenate([blockdiag(u[i]), bd_v[i]], axis=0)) for i in range(len(items))]
    for i, (d, gi) in enumerate(items):
        uv = jnp.concatenate([u[i].astype(BF16), pre[d]["v"][:, cols(gi)].astype(BF16)], axis=0)
        sbk = jnp.concatenate([pre[d]["s_b"][:, cols(gi)], pre[d]["s_k"][:, cols(gi)]], axis=0)
        upd = lax.dot_general(uv, sbk, tn, preferred_element_type=F32)
        s_scr[d, gi] = s_old[i] * pre[d]["p_tot"][:, cols(gi)] + jnp.where(state_mask, upd, 0.0)

    visits = ((row_f, jnp.concatenate(y[:n_groups], axis=1), pre[0]["bonus"]),
              (row_b, jnp.concatenate(y[n_groups:], axis=1), pre[1]["bonus"]))

    @pl.when(jnp.logical_not(second_visit))
    def _():
        for rows, y_d, bonus_d in visits:
            y_ref[pl.ds(rows, c), :] = y_d
            bonus_scr[pl.ds(rows, c), :] = bonus_d

    @pl.when(second_visit)
    def _():
        for rows, y_d, bonus_d in visits:
            y_sum = y_ref[pl.ds(rows, c), :] + y_d
            y_ref[pl.ds(rows, c), :] = (_group_norm(y_sum, bd_ref[...], lnw_ref[...], lnb_ref[...])
                                         + bonus_scr[pl.ds(rows, c), :] + bonus_d)

    @pl.when(jnp.logical_and(cidx == n_chunks - 1, is_ctx))
    def _():
        for d, out_ref in enumerate((sf_ref, sb_ref)):
            for h in range(N_HEADS):
                gi, l = divmod(h, g)
                hs = slice(l * HEAD_DIM, (l + 1) * HEAD_DIM)
                out_ref[h] = s_scr[d, gi, hs, hs]


def _scan_row_block(s, reverse):
    ctx = s < CTX_STEPS
    n_chunks = jnp.where(ctx, CTX_CHUNKS, LAT_CHUNKS)
    t = jnp.where(ctx, s, s - CTX_STEPS)
    cidx = t & (n_chunks - 1)
    if reverse:
        cidx = n_chunks - 1 - cidx
    base = jnp.where(ctx, 0, CTX_STEPS)
    return base + (t - (t & (n_chunks - 1))) + cidx


def _scan_call(u, s0f, s0b, w0, wup, a0, aup, k_k, k_a, r_k, bd, lnw, lnb):
    assert CTX_STEPS // N_CTX_SS == LAT_CHUNKS, "context and latent super sequences take the same number of steps"
    steps_per_ss = LAT_CHUNKS
    fwd = lambda s: (_scan_row_block(s, False), 0)
    bwd = lambda s: (_scan_row_block(s, True), 0)
    lat_seq = lambda s: (jnp.maximum(s - CTX_STEPS, 0) // LAT_CHUNKS, 0, 0, 0)
    ctx_seq = lambda s: (jnp.minimum(s // CTX_CHUNKS, CTX_B - 1), 0, 0, 0)
    const2 = lambda s: (0, 0)
    const3 = lambda s: (0, 0, 0)
    st_block = (None, N_HEADS, HEAD_DIM, HEAD_DIM)
    y_sds = jax.ShapeDtypeStruct((N_TOK, D), F32)
    s_sds = jax.ShapeDtypeStruct((CTX_B, N_HEADS, HEAD_DIM, HEAD_DIM), F32)
    return pl.pallas_call(
        _scan_kernel,
        out_shape=(y_sds, s_sds, s_sds),
        grid=(CTX_STEPS + LAT_STEPS,),
        in_specs=[
            pl.BlockSpec((CHUNK, N_RWKV_COLS), fwd),
            pl.BlockSpec((CHUNK, N_RWKV_COLS), bwd),
            pl.BlockSpec(st_block, lat_seq),
            pl.BlockSpec(st_block, lat_seq),
            pl.BlockSpec((2, 1, D), const3),
            pl.BlockSpec((2, LORA_W, D), const3),
            pl.BlockSpec((2, 1, D), const3),
            pl.BlockSpec((2, LORA_A, D), const3),
            pl.BlockSpec((1, D), const2),
            pl.BlockSpec((1, D), const2),
            pl.BlockSpec((1, D), const2),
            pl.BlockSpec((STAT_W, STAT_W), const2),
            pl.BlockSpec((1, D), const2),
            pl.BlockSpec((1, D), const2),
        ],
        out_specs=(
            pl.BlockSpec((SS_LEN, D), lambda s: (s // steps_per_ss, 0)),
            pl.BlockSpec(st_block, ctx_seq),
            pl.BlockSpec(st_block, ctx_seq),
        ),
        scratch_shapes=[
            pltpu.VMEM((2, N_HEADS // HEAD_GROUP, HEAD_GROUP * HEAD_DIM, HEAD_GROUP * HEAD_DIM), F32),
            pltpu.VMEM((SS_LEN, D), F32),
        ],
        compiler_params=_cparams(("arbitrary",)),
        name="wkv7_scan",
    )(u, u, s0f, s0b, w0, wup, a0, aup, k_k, k_a, r_k, bd, lnw, lnb)


def _convmix_kernel(z_ref, zp_ref, zn_ref, w_ref, b_ref, lnw_ref, lnb_ref, x_ref, mod_ref, y_ref, gd_ref,
                    ga_ref, gb_ref, gup_ref, wo_ref, o_ref, pad_scr, cv_scr, *, tm, halo):
    i = pl.program_id(0)
    tiles_per_lat = LAT_T // tm
    j = (i - N_CTX // tm) % tiles_per_lat
    is_ctx = i < N_CTX // tm
    keep_prev = jnp.where(jnp.logical_or(is_ctx, j == 0), 0.0, 1.0)
    keep_next = jnp.where(jnp.logical_or(is_ctx, j == tiles_per_lat - 1), 0.0, 1.0)
    pad_scr[0, 0:halo, :] = zp_ref[...] * keep_prev
    pad_scr[0, halo:halo + tm, :] = z_ref[...]
    pad_scr[0, halo + tm:halo + tm + halo, :] = zn_ref[...] * keep_next
    span = tm + 2 * halo - SUBLANES
    for m in range(1, SUBLANES):
        pad_scr[m, 0:span, :] = pad_scr[0, m:m + span, :]

    rb = 128
    lanes = 128

    def lane_block(cb, carry):
        cs = pl.ds(pl.multiple_of(cb * lanes, lanes), lanes)
        for r0 in range(0, tm, rb):
            acc = jnp.broadcast_to(b_ref[:, cs], (rb, lanes))
            for tap in range(CONV_WIDTH):
                q, m = divmod(tap + halo - CONV_HALF, SUBLANES)
                start = r0 + q * SUBLANES
                acc = acc + w_ref[tap:tap + 1, cs] * pad_scr[m, start:start + rb, cs]
            cv_scr[r0:r0 + rb, cs] = acc
        return carry

    lax.fori_loop(0, D // lanes, lane_block, 0)
    y = cv_scr[...]
    mean = jnp.mean(y, axis=-1, keepdims=True)
    yc = y - mean
    var = jnp.mean(yc * yc, axis=-1, keepdims=True)
    y = yc * lax.rsqrt(var + LN_EPS) * lnw_ref[...] + lnb_ref[...]
    y_conv = y * _sigmoid(y)

    g = _dot(_sigmoid(gd_ref[...]).astype(BF16), gup_ref[...])
    mixed = ga_ref[...] * (y_ref[...] * g) + gb_ref[...] * y_conv
    o_ref[...] = x_ref[...] + mod_ref[5:6, :] * _dot(mixed.astype(BF16), wo_ref[...])


def _convmix_call(z, w, b, lnw, lnb, x, modss, y_att, u, ga, gb, g_up, w_out, *, tm=256, halo=16):
    hb = tm // halo
    n_h = N_TOK // halo
    ss_per = SS_LEN // tm
    tok = lambda i: (i, 0)
    const2 = lambda i: (0, 0)
    gd_blk = (3 * D + 2 * LORA_W + 2 * LORA_A) // LORA_G
    kern = functools.partial(_convmix_kernel, tm=tm, halo=halo)
    return pl.pallas_call(
        kern,
        out_shape=jax.ShapeDtypeStruct((N_TOK, D), F32),
        grid=(N_TOK // tm,),
        in_specs=[
            pl.BlockSpec((tm, D), tok),
            pl.BlockSpec((halo, D), lambda i: (jnp.maximum(i * hb - 1, 0), 0)),
            pl.BlockSpec((halo, D), lambda i: (jnp.minimum((i + 1) * hb, n_h - 1), 0)),
            pl.BlockSpec((CONV_WIDTH, D), const2),
            pl.BlockSpec((1, D), const2),
            pl.BlockSpec((1, D), const2),
            pl.BlockSpec((1, D), const2),
            pl.BlockSpec((tm, D), tok),
            pl.BlockSpec((None, N_MOD, D), lambda i: (i // ss_per, 0, 0)),
            pl.BlockSpec((tm, D), tok),
            pl.BlockSpec((tm, LORA_G), lambda i: (i, gd_blk)),
            pl.BlockSpec((tm, D), tok),
            pl.BlockSpec((tm, D), tok),
            pl.BlockSpec((LORA_G, D), const2),
            pl.BlockSpec((D, D), const2),
        ],
        out_specs=pl.BlockSpec((tm, D), tok),
        scratch_shapes=[pltpu.VMEM((SUBLANES, tm + 2 * halo, D), F32), pltpu.VMEM((tm, D), F32)],
        compiler_params=_cparams(("parallel",)),
        name="conv_mixout",
    )(z, z, z, w, b, lnw, lnb, x, modss, y_att, u, ga, gb, g_up, w_out)


def _grid_pos_embed(rows):
    n_freq = D // 4
    freqs = jnp.exp(-math.log(POS_MAX_PERIOD) * jnp.arange(n_freq, dtype=F32) / n_freq)
    ang_r = jnp.arange(rows, dtype=F32)[:, None] * freqs
    ang_c = jnp.arange(GRID_W, dtype=F32)[:, None] * freqs
    emb_r = jnp.concatenate([jnp.sin(ang_r), jnp.cos(ang_r)], axis=-1)
    emb_c = jnp.concatenate([jnp.sin(ang_c), jnp.cos(ang_c)], axis=-1)
    return jnp.concatenate([jnp.repeat(emb_r, GRID_W, axis=0), jnp.tile(emb_c, (rows, 1))], axis=-1)


def kernel(x_prompt, x_sample, state_rwkv_fwd, state_rwkv_bwd, c, c_ctx, w_mod, b_mod, norm_ffn1, ffn1_w_in, ffn1_w_out, norm_mix, mix_w_in, rwkv_mu, rwkv_w0, rwkv_w_up, rwkv_a0, rwkv_a_up, rwkv_g_up, rwkv_k_k, rwkv_k_a, rwkv_r_k, rwkv_lnx_w, rwkv_lnx_b, conv_dw, conv_db, conv_ln_w, conv_ln_b, mix_w_out, norm_ffn2, ffn2_w_in, ffn2_w_out, norm_final):
    assert x_prompt.shape == (CTX_B, CTX_T, D) and x_sample.shape == (LAT_B, LAT_T, D)
    assert w_mod.shape[0] == 1, "single-layer trunk"
    row = lambda t: t.reshape(1, -1)

    cond = jnp.concatenate([c, c_ctx[None], jnp.zeros((8 - LAT_B - 1, D), F32)], axis=0)
    mod = _mod_call(cond, w_mod[0], row(b_mod[0])).reshape(8, N_MOD, D)
    ss_cond = jnp.array([LAT_B] * N_CTX_SS + list(range(LAT_B)), jnp.int32)
    modss = mod[ss_cond]

    pos = _grid_pos_embed(LAT_T // GRID_W).astype(x_sample.dtype)

    w_mix = mix_w_in[0]
    glu_off = N_RWKV_COLS
    gate_off = N_RWKV_COLS + 2 * D
    col_pad = ((0, 0), (0, N_RWKV_PAD - N_RWKV_COLS))
    w_rwkv = jnp.pad(w_mix[:, :glu_off], col_pad).astype(BF16)
    w_glu = w_mix[:, glu_off:gate_off].astype(BF16)
    w_gate = w_mix[:, gate_off:].astype(BF16)

    x1 = _ffn_first_call(x_prompt.reshape(N_CTX, D), x_sample.reshape(LAT_B * LAT_T, D), pos, modss,
                         row(norm_ffn1[0]), ffn1_w_in[0], ffn1_w_out[0])

    u = _mixin_rwkv_call(x1, modss, row(norm_mix[0]), w_rwkv, jnp.pad(row(rwkv_mu[0]), col_pad))
    z, ga, gb = _mixin_gate_call(x1, modss, row(norm_mix[0]), w_glu, w_gate)

    ii = lax.broadcasted_iota(jnp.int32, (STAT_W, STAT_W), 0) // HEAD_DIM
    jj = lax.broadcasted_iota(jnp.int32, (STAT_W, STAT_W), 1) // HEAD_DIM
    bd = jnp.where(ii == jj, 1.0, 0.0).astype(BF16)

    y_att, s_f, s_b = _scan_call(
        u, state_rwkv_fwd[:, 0], state_rwkv_bwd[:, 0],
        rwkv_w0[0].reshape(2, 1, D), rwkv_w_up[0].astype(BF16), rwkv_a0[0].reshape(2, 1, D),
        rwkv_a_up[0].astype(BF16), row(rwkv_k_k[0]), row(rwkv_k_a[0]), row(rwkv_r_k[0]), bd,
        row(rwkv_lnx_w[0]), row(rwkv_lnx_b[0]))

    x2 = _convmix_call(z, conv_dw[0], row(conv_db[0]), row(conv_ln_w[0]), row(conv_ln_b[0]),
                       x1, modss, y_att, u, ga, gb, rwkv_g_up[0].astype(BF16), mix_w_out[0].astype(BF16))

    y_ctx, y_lat = _ffn_last_call(x2, modss, row(norm_ffn2[0]), ffn2_w_in[0], ffn2_w_out[0], row(norm_final))

    st = (CTX_B, 1, N_HEADS, HEAD_DIM, HEAD_DIM)
    return (y_ctx.reshape(CTX_B, CTX_T, D), y_lat.reshape(LAT_B, LAT_T, D),
            s_f.reshape(st).astype(x_prompt.dtype), s_b.reshape(st).astype(x_prompt.dtype))
```

```python
import functools
import math

import jax
import jax.numpy as jnp
from jax import lax
from jax.experimental import pallas as pl
from jax.experimental.pallas import tpu as pltpu

F32 = jnp.float32
BF16 = jnp.bfloat16

D = 1024
N_HEADS = 16
HEAD_DIM = 64
D_FF = 4 * D
N_MOD = 9
CONV_WIDTH = 31
CONV_HALF = CONV_WIDTH // 2
LORA_W = 64
LORA_A = 64
LORA_G = 128
N_RWKV_COLS = 3 * D + 2 * LORA_W + 2 * LORA_A + LORA_G
N_RWKV_PAD = 3584
GRID_W = 64
EPS = 1e-6
LN_EPS = 1e-5
GN_EPS = 64e-5
DECAY_SCALE = 0.606531
POS_MAX_PERIOD = 10000.0

CTX_B, CTX_T = 32, 256
LAT_B, LAT_T = 4, 2048
N_CTX = CTX_B * CTX_T
N_TOK = N_CTX + LAT_B * LAT_T
SS_LEN = 2048
N_SS = N_TOK // SS_LEN
N_CTX_SS = N_CTX // SS_LEN

CHUNK = 64
HEAD_GROUP = 2
CTX_CHUNKS = CTX_T // CHUNK
LAT_CHUNKS = LAT_T // CHUNK
CTX_STEPS = CTX_B * CTX_CHUNKS
LAT_STEPS = LAT_B * LAT_CHUNKS
STAT_W = 256
NORM_ROWS = 256
SUBLANES = 8

VMEM_LIMIT = 58 * 1024 * 1024


def _cparams(sem):
    return pltpu.CompilerParams(dimension_semantics=sem, vmem_limit_bytes=VMEM_LIMIT)


def _sigmoid(x):
    return 1.0 / (1.0 + jnp.exp(-x))


def _dot(a, b):
    return jnp.dot(a, b, preferred_element_type=F32)


def _split_bf16(x):
    hi = x.astype(BF16)
    lo = (x - hi.astype(F32)).astype(BF16)
    return hi, lo


def _head_stat(x, bd):
    groups = D // STAT_W
    stacked = jnp.concatenate([x[:, g * STAT_W:(g + 1) * STAT_W] for g in range(groups)], axis=0)
    s = _dot(stacked.astype(BF16), bd)
    rows = x.shape[0]
    return jnp.concatenate([s[g * rows:(g + 1) * rows] for g in range(groups)], axis=1)


def _mod_kernel(cond_ref, w_ref, b_ref, o_ref):
    c = cond_ref[...]
    s = (c * _sigmoid(c)).astype(BF16)
    o_ref[...] = _dot(s, w_ref[...].astype(BF16)) + b_ref[...]


def _mod_call(cond, w_mod, b_mod):
    tn = 1152
    n = N_MOD * D
    return pl.pallas_call(
        _mod_kernel,
        out_shape=jax.ShapeDtypeStruct((8, n), F32),
        grid=(n // tn,),
        in_specs=[
            pl.BlockSpec((8, D), lambda j: (0, 0)),
            pl.BlockSpec((D, tn), lambda j: (0, j)),
            pl.BlockSpec((1, tn), lambda j: (0, j)),
        ],
        out_specs=pl.BlockSpec((8, tn), lambda j: (0, j)),
        compiler_params=_cparams(("arbitrary",)),
        name="mod",
    )(cond, w_mod, b_mod)


def _rms(x, gain):
    return x * lax.rsqrt(jnp.mean(x * x, axis=-1, keepdims=True) + EPS) * gain


def _ffn_start(x, mod_ref, gain_ref, h_scr, acc_scr, mod_row):
    y = _rms(x, gain_ref[...])
    h = y * (1.0 + mod_ref[mod_row + 1:mod_row + 2, :]) + mod_ref[mod_row:mod_row + 1, :]
    h_scr[...] = h.astype(BF16)
    acc_scr[...] = jnp.zeros_like(acc_scr)


def _ffn_step(wg_ref, wu_ref, wo_ref, h_scr, acc_scr):
    h = h_scr[...]
    g = _dot(h, wg_ref[...].astype(BF16))
    u = _dot(h, wu_ref[...].astype(BF16))
    a = (g * _sigmoid(g) * u).astype(BF16)
    acc_scr[...] += _dot(a, wo_ref[...].astype(BF16))


def _ffn_first_kernel(xc_ref, xl_ref, pos_ref, mod_ref, gain_ref, wg_ref, wu_ref, wo_ref, o_ref,
                      h_scr, acc_scr, *, n_f, ctx_tiles):
    i = pl.program_id(0)
    f = pl.program_id(1)
    is_ctx = i < ctx_tiles
    sources = ((is_ctx, lambda: xc_ref[...]), (jnp.logical_not(is_ctx), lambda: xl_ref[...] + pos_ref[...]))

    for cond, x_in in sources:
        @pl.when(jnp.logical_and(f == 0, cond))
        def _():
            _ffn_start(x_in(), mod_ref, gain_ref, h_scr, acc_scr, 0)

    _ffn_step(wg_ref, wu_ref, wo_ref, h_scr, acc_scr)

    for cond, x_in in sources:
        @pl.when(jnp.logical_and(f == n_f - 1, cond))
        def _():
            o_ref[...] = x_in() + mod_ref[2:3, :] * (0.5 * acc_scr[...])


def _ffn_last_kernel(x_ref, mod_ref, gain_ref, wg_ref, wu_ref, wo_ref, gain2_ref, oc_ref, ol_ref,
                     h_scr, acc_scr, *, n_f, ctx_tiles):
    i = pl.program_id(0)
    f = pl.program_id(1)

    @pl.when(f == 0)
    def _():
        _ffn_start(x_ref[...], mod_ref, gain_ref, h_scr, acc_scr, 6)

    _ffn_step(wg_ref, wu_ref, wo_ref, h_scr, acc_scr)

    def result():
        return _rms(x_ref[...] + mod_ref[8:9, :] * (0.5 * acc_scr[...]), gain2_ref[...])

    @pl.when(jnp.logical_and(f == n_f - 1, i < ctx_tiles))
    def _():
        oc_ref[...] = result()

    @pl.when(jnp.logical_and(f == n_f - 1, i >= ctx_tiles))
    def _():
        ol_ref[...] = result()


def _ffn_specs(tm, tf):
    n_f = D_FF // tf
    ss_per = SS_LEN // tm
    return [
        pl.BlockSpec((None, N_MOD, D), lambda i, f: (i // ss_per, 0, 0)),
        pl.BlockSpec((1, D), lambda i, f: (0, 0)),
        pl.BlockSpec((D, tf), lambda i, f: (0, f)),
        pl.BlockSpec((D, tf), lambda i, f: (0, n_f + f)),
        pl.BlockSpec((tf, D), lambda i, f: (f, 0)),
    ]


def _ffn_first_call(xc, xl, pos, modss, gain, w_in, w_out, *, tm=1024, tf=512):
    n_f = D_FF // tf
    ctx_tiles = N_CTX // tm
    pos_tiles = LAT_T // tm
    kern = functools.partial(_ffn_first_kernel, n_f=n_f, ctx_tiles=ctx_tiles)
    return pl.pallas_call(
        kern,
        out_shape=jax.ShapeDtypeStruct((N_TOK, D), F32),
        grid=(N_TOK // tm, n_f),
        in_specs=[
            pl.BlockSpec((tm, D), lambda i, f: (jnp.minimum(i, ctx_tiles - 1), 0)),
            pl.BlockSpec((tm, D), lambda i, f: (jnp.maximum(i - ctx_tiles, 0), 0)),
            pl.BlockSpec((tm, D), lambda i, f: (jnp.maximum(i - ctx_tiles, 0) % pos_tiles, 0)),
        ] + _ffn_specs(tm, tf),
        out_specs=pl.BlockSpec((tm, D), lambda i, f: (i, 0)),
        scratch_shapes=[pltpu.VMEM((tm, D), BF16), pltpu.VMEM((tm, D), F32)],
        compiler_params=_cparams(("parallel", "arbitrary")),
        name="ffn_first",
    )(xc, xl, pos, modss, gain, w_in, w_in, w_out)


def _ffn_last_call(x, modss, gain, w_in, w_out, gain2, *, tm=1024, tf=512):
    n_f = D_FF // tf
    ctx_tiles = N_CTX // tm
    kern = functools.partial(_ffn_last_kernel, n_f=n_f, ctx_tiles=ctx_tiles)
    return pl.pallas_call(
        kern,
        out_shape=(jax.ShapeDtypeStruct((N_CTX, D), F32), jax.ShapeDtypeStruct((N_TOK - N_CTX, D), F32)),
        grid=(N_TOK // tm, n_f),
        in_specs=[pl.BlockSpec((tm, D), lambda i, f: (i, 0))] + _ffn_specs(tm, tf) + [
            pl.BlockSpec((1, D), lambda i, f: (0, 0))],
        out_specs=(pl.BlockSpec((tm, D), lambda i, f: (jnp.minimum(i, ctx_tiles - 1), 0)),
                   pl.BlockSpec((tm, D), lambda i, f: (jnp.maximum(i - ctx_tiles, 0), 0))),
        scratch_shapes=[pltpu.VMEM((tm, D), BF16), pltpu.VMEM((tm, D), F32)],
        compiler_params=_cparams(("parallel", "arbitrary")),
        name="ffn_last",
    )(x, modss, gain, w_in, w_in, w_out, gain2)


def _mixin_rwkv_kernel(x_ref, mod_ref, gain_ref, w_ref, mu_ref, keep_prev_ref, keep_next_ref, o_ref, h_scr):
    j = pl.program_id(1)

    @pl.when(j == 0)
    def _():
        y = _rms(x_ref[...], gain_ref[...])
        h_scr[...] = (y * (1.0 + mod_ref[4:5, :]) + mod_ref[3:4, :]).astype(BF16)

    u = _dot(h_scr[...], w_ref[...])
    rows, cols = u.shape
    keep_prev = keep_prev_ref[...]
    keep_next = keep_next_ref[...]
    lanes = keep_prev.shape[1]
    for l in range(cols // lanes):
        ls = slice(l * lanes, (l + 1) * lanes)
        ul = u[:, ls]
        neighbours = pltpu.roll(ul, 1, 0) * keep_prev + pltpu.roll(ul, rows - 1, 0) * keep_next
        mu = mu_ref[:, ls]
        o_ref[:, ls] = ul * (1.0 - mu) + neighbours * (0.5 * mu)


def _mixin_rwkv_call(x, modss, gain, w, mu, *, tn=512, lanes=128):
    tm = SS_LEN
    pos = lax.broadcasted_iota(jnp.int32, (2, tm, lanes), 1)
    seq_len = jnp.array([CTX_T, LAT_T], jnp.int32)[:, None, None]
    pos = pos & (seq_len - 1)
    keep_prev = jnp.where(pos == 0, 0.0, 1.0)
    keep_next = jnp.where(pos == seq_len - 1, 0.0, 1.0)
    kind = lambda i, j: (jnp.where(i < N_CTX_SS, 0, 1), 0, 0)
    return pl.pallas_call(
        _mixin_rwkv_kernel,
        out_shape=jax.ShapeDtypeStruct((N_TOK, N_RWKV_PAD), F32),
        grid=(N_TOK // tm, N_RWKV_PAD // tn),
        in_specs=[
            pl.BlockSpec((tm, D), lambda i, j: (i, 0)),
            pl.BlockSpec((None, N_MOD, D), lambda i, j: (i, 0, 0)),
            pl.BlockSpec((1, D), lambda i, j: (0, 0)),
            pl.BlockSpec((D, tn), lambda i, j: (0, j)),
            pl.BlockSpec((1, tn), lambda i, j: (0, j)),
            pl.BlockSpec((None, tm, lanes), kind),
            pl.BlockSpec((None, tm, lanes), kind),
        ],
        out_specs=pl.BlockSpec((tm, tn), lambda i, j: (i, j)),
        scratch_shapes=[pltpu.VMEM((tm, D), BF16)],
        compiler_params=_cparams(("parallel", "arbitrary")),
        name="mixin_rwkv",
    )(x, modss, gain, w, mu, keep_prev, keep_next)


def _mixin_gate_kernel(x_ref, mod_ref, gain_ref, wa_ref, wb_ref, wga_ref, wgb_ref, z_ref, ga_ref, gb_ref,
                       h_scr):
    j = pl.program_id(1)

    @pl.when(j == 0)
    def _():
        y = _rms(x_ref[...], gain_ref[...])
        h_scr[...] = (y * (1.0 + mod_ref[4:5, :]) + mod_ref[3:4, :]).astype(BF16)

    h = h_scr[...]
    z_ref[...] = _dot(h, wa_ref[...]) * _sigmoid(_dot(h, wb_ref[...]))
    ga_ref[...] = _sigmoid(_dot(h, wga_ref[...]))
    gb_ref[...] = _sigmoid(_dot(h, wgb_ref[...]))


def _mixin_gate_call(x, modss, gain, w_glu, w_gate, *, tm=1024, tn=512):
    n_j = D // tn
    ss_per = SS_LEN // tm
    sds = jax.ShapeDtypeStruct((N_TOK, D), F32)
    lo = lambda i, j: (0, j)
    hi = lambda i, j: (0, n_j + j)
    out = pl.BlockSpec((tm, tn), lambda i, j: (i, j))
    return pl.pallas_call(
        _mixin_gate_kernel,
        out_shape=(sds, sds, sds),
        grid=(N_TOK // tm, n_j),
        in_specs=[
            pl.BlockSpec((tm, D), lambda i, j: (i, 0)),
            pl.BlockSpec((None, N_MOD, D), lambda i, j: (i // ss_per, 0, 0)),
            pl.BlockSpec((1, D), lambda i, j: (0, 0)),
            pl.BlockSpec((D, tn), lo),
            pl.BlockSpec((D, tn), hi),
            pl.BlockSpec((D, tn), lo),
            pl.BlockSpec((D, tn), hi),
        ],
        out_specs=(out, out, out),
        scratch_shapes=[pltpu.VMEM((tm, D), BF16)],
        compiler_params=_cparams(("parallel", "arbitrary")),
        name="mixin_gate",
    )(x, modss, gain, w_glu, w_glu, w_gate, w_gate)


def _scan_prep(x, d, reverse, prm):
    (w0_ref, wup_ref, a0_ref, aup_ref, kk_ref, ka_ref, rk_ref, bd_ref) = prm
    c = CHUNK
    r = x[:, 0:D]
    k = x[:, D:2 * D]
    v = x[:, 2 * D:3 * D]
    wd = x[:, 3 * D + d * LORA_W:3 * D + (d + 1) * LORA_W]
    ad_off = 3 * D + 2 * LORA_W
    ad = x[:, ad_off + d * LORA_A:ad_off + (d + 1) * LORA_A]
    bd = bd_ref[...]

    logw = -DECAY_SCALE * _sigmoid(w0_ref[d] + _dot(jnp.tanh(wd).astype(BF16), wup_ref[d]))
    a = _sigmoid(a0_ref[d] + _dot(ad.astype(BF16), aup_ref[d]))
    kk = k * kk_ref[...]
    kk = kk * lax.rsqrt(_head_stat(kk * kk, bd) + 1e-12)
    kd = k * (1.0 + (a - 1.0) * ka_ref[...])
    b = kk * a
    bonus = _head_stat(r * kd * rk_ref[...], bd) * v

    ti = lax.broadcasted_iota(jnp.int32, (c, c), 0)
    si = lax.broadcasted_iota(jnp.int32, (c, c), 1)
    tri = jnp.where((ti <= si) if reverse else (ti >= si), 1.0, 0.0).astype(BF16)
    lw_hi, lw_lo = _split_bf16(logw)
    cum = _dot(tri, lw_hi) + _dot(tri, lw_lo)
    tot = cum[0:1, :] if reverse else cum[c - 1:c, :]
    p_inv = jnp.exp(-cum)
    p_tot = jnp.exp(tot)
    r_b = b * p_inv
    r_k = kd * p_inv
    return dict(
        l_kk=(kk * jnp.exp(cum - logw)).astype(BF16), l_r=(r * jnp.exp(cum)).astype(BF16),
        r_b=r_b, r_k=r_k, s_b=(r_b * p_tot).astype(BF16), s_k=(r_k * p_tot).astype(BF16),
        v=v, p_tot=p_tot, bonus=bonus)


def _group_norm(y, bd, lnw, lnb):
    yc = y - _head_stat(y, bd) * (1.0 / HEAD_DIM)
    var = _head_stat(yc * yc, bd) * (1.0 / HEAD_DIM)
    return yc * lax.rsqrt(var + GN_EPS) * lnw + lnb


def _scan_kernel(xf_ref, xb_ref, s0f_ref, s0b_ref, w0_ref, wup_ref, a0_ref, aup_ref, kk_ref, ka_ref,
                 rk_ref, bd_ref, lnw_ref, lnb_ref, y_ref, sf_ref, sb_ref, s_scr, bonus_scr):
    s = pl.program_id(0)
    is_ctx = s < CTX_STEPS
    n_chunks = jnp.where(is_ctx, CTX_CHUNKS, LAT_CHUNKS)
    t_step = jnp.where(is_ctx, s, s - CTX_STEPS)
    cidx = t_step & (n_chunks - 1)
    seq_in_ss = jnp.where(is_ctx, (t_step >> (CTX_CHUNKS.bit_length() - 1)) & (SS_LEN // CTX_T - 1), 0)
    row_f = pl.multiple_of((seq_in_ss * n_chunks + cidx) * CHUNK, CHUNK)
    row_b = pl.multiple_of((seq_in_ss * n_chunks + n_chunks - 1 - cidx) * CHUNK, CHUNK)
    second_visit = cidx >= (n_chunks >> 1)
    prm = (w0_ref, wup_ref, a0_ref, aup_ref, kk_ref, ka_ref, rk_ref, bd_ref)
    c, g, gw = CHUNK, HEAD_GROUP, HEAD_GROUP * HEAD_DIM
    n_groups = N_HEADS // g
    s0_refs = (s0f_ref, s0b_ref)

    @pl.when(jnp.logical_and(cidx == 0, is_ctx))
    def _():
        s_scr[...] = jnp.zeros_like(s_scr)

    @pl.when(jnp.logical_and(cidx == 0, jnp.logical_not(is_ctx)))
    def _():
        for d in range(2):
            for gi in range(n_groups):
                for l in range(g):
                    pieces = [jnp.zeros((HEAD_DIM, HEAD_DIM), F32)] * g
                    pieces[l] = s0_refs[d][gi * g + l]
                    s_scr[d, gi, l * HEAD_DIM:(l + 1) * HEAD_DIM, :] = jnp.concatenate(pieces, axis=1)

    pre = (_scan_prep(xf_ref[...], 0, False, prm), _scan_prep(xb_ref[...], 1, True, prm))

    row = lax.broadcasted_iota(jnp.int32, (c, gw), 0)
    lane = lax.broadcasted_iota(jnp.int32, (c, gw), 1)
    col = lane & (HEAD_DIM - 1)
    head_of_lane = lane >> 6
    eye = jnp.where(row == col, 1.0, 0.0)
    strict = (row > col, row < col)
    incl = (row >= col, row <= col)
    srow = lax.broadcasted_iota(jnp.int32, (gw, gw), 0) >> 6
    slane = lax.broadcasted_iota(jnp.int32, (gw, gw), 1) >> 6
    state_mask = srow == slane
    nt = (((1,), (1,)), ((), ()))
    tn = (((0,), (0,)), ((), ()))

    def blockdiag(x):
        return jnp.concatenate([jnp.where(head_of_lane == l, x, 0.0) for l in range(g)], axis=0).astype(BF16)

    def same_block(n):
        shift = n.bit_length() - 1
        return (row >> shift) == (col >> shift)

    items = [(d, gi) for d in range(2) for gi in range(n_groups)]
    cols = lambda gi: slice(gi * gw, (gi + 1) * gw)

    s_old = [s_scr[d, gi] for d, gi in items]
    lhs = [jnp.concatenate([pre[d]["l_kk"][:, cols(gi)], pre[d]["l_r"][:, cols(gi)]], axis=0) for d, gi in items]
    rt = [jnp.concatenate([blockdiag(pre[d]["r_b"][:, cols(gi)]), blockdiag(pre[d]["r_k"][:, cols(gi)])], axis=0)
          for d, gi in items]
    amat = [lax.dot_general(lhs[i], rt[i], nt, preferred_element_type=F32) for i in range(len(items))]
    ls = [lax.dot_general(lhs[i], s_old[i].astype(BF16), nt, preferred_element_type=F32) for i in range(len(items))]
    a_ab = [jnp.where(strict[d], amat[i][:c, :gw], 0.0) for i, (d, gi) in enumerate(items)]
    a_ak = [jnp.where(strict[d], amat[i][:c, gw:], 0.0).astype(BF16) for i, (d, gi) in enumerate(items)]
    a_r = [jnp.concatenate([jnp.where(incl[d], amat[i][c:, :gw], 0.0), jnp.where(incl[d], amat[i][c:, gw:], 0.0)],
                           axis=1).astype(BF16) for i, (d, gi) in enumerate(items)]
    bd_v = [blockdiag(pre[d]["v"][:, cols(gi)]) for d, gi in items]
    rhs = [-(ls[i][:c] + _dot(a_ak[i], bd_v[i])) for i in range(len(items))]

    tinv = [eye - jnp.where(same_block(2), a, 0.0) for a in a_ab]
    n = 2
    while n < c:
        lower = jnp.logical_and(same_block(2 * n), jnp.logical_not(same_block(n)))
        off = [jnp.where(lower, a, 0.0).astype(BF16) for a in a_ab]
        w = [_dot(off[i], blockdiag(tinv[i])) for i in range(len(items))]
        tinv = [tinv[i] - _dot(tinv[i].astype(BF16), blockdiag(w[i])) for i in range(len(items))]
        n *= 2

    u = [_dot(tinv[i].astype(BF16), blockdiag(rhs[i])) for i in range(len(items))]
    y = [ls[i][c:] + _dot(a_r[i], jnp.concatenate([blockdiag(u[i]), bd_v[i]], axis=0)) for i in range(len(items))]
    for i, (d, gi) in enumerate(items):
        uv = jnp.concatenate([u[i].astype(BF16), pre[d]["v"][:, cols(gi)].astype(BF16)], axis=0)
        sbk = jnp.concatenate([pre[d]["s_b"][:, cols(gi)], pre[d]["s_k"][:, cols(gi)]], axis=0)
        upd = lax.dot_general(uv, sbk, tn, preferred_element_type=F32)
        s_scr[d, gi] = s_old[i] * pre[d]["p_tot"][:, cols(gi)] + jnp.where(state_mask, upd, 0.0)

    visits = ((row_f, jnp.concatenate(y[:n_groups], axis=1), pre[0]["bonus"]),
              (row_b, jnp.concatenate(y[n_groups:], axis=1), pre[1]["bonus"]))

    @pl.when(jnp.logical_not(second_visit))
    def _():
        for rows, y_d, bonus_d in visits:
            y_ref[pl.ds(rows, c), :] = y_d
            bonus_scr[pl.ds(rows, c), :] = bonus_d

    @pl.when(second_visit)
    def _():
        for rows, y_d, bonus_d in visits:
            y_ref[pl.ds(rows, c), :] = y_ref[pl.ds(rows, c), :] + y_d
            bonus_scr[pl.ds(rows, c), :] = bonus_scr[pl.ds(rows, c), :] + bonus_d

    @pl.when(s % (SS_LEN // CHUNK) == SS_LEN // CHUNK - 1)
    def _():
        def norm_block(i, carry):
            rows = pl.ds(pl.multiple_of(i * NORM_ROWS, NORM_ROWS), NORM_ROWS)
            y_ref[rows, :] = (_group_norm(y_ref[rows, :], bd_ref[...], lnw_ref[...], lnb_ref[...])
                              + bonus_scr[rows, :])
            return carry

        lax.fori_loop(0, SS_LEN // NORM_ROWS, norm_block, 0)

    @pl.when(jnp.logical_and(cidx == n_chunks - 1, is_ctx))
    def _():
        for d, out_ref in enumerate((sf_ref, sb_ref)):
            for h in range(N_HEADS):
                gi, l = divmod(h, g)
                hs = slice(l * HEAD_DIM, (l + 1) * HEAD_DIM)
                out_ref[h] = s_scr[d, gi, hs, hs]


def _scan_row_block(s, reverse):
    ctx = s < CTX_STEPS
    n_chunks = jnp.where(ctx, CTX_CHUNKS, LAT_CHUNKS)
    t = jnp.where(ctx, s, s - CTX_STEPS)
    cidx = t & (n_chunks - 1)
    if reverse:
        cidx = n_chunks - 1 - cidx
    base = jnp.where(ctx, 0, CTX_STEPS)
    return base + (t - (t & (n_chunks - 1))) + cidx


def _scan_call(u, s0f, s0b, w0, wup, a0, aup, k_k, k_a, r_k, bd, lnw, lnb):
    assert CTX_STEPS // N_CTX_SS == LAT_CHUNKS, "context and latent super sequences take the same number of steps"
    steps_per_ss = LAT_CHUNKS
    fwd = lambda s: (_scan_row_block(s, False), 0)
    bwd = lambda s: (_scan_row_block(s, True), 0)
    lat_seq = lambda s: (jnp.maximum(s - CTX_STEPS, 0) // LAT_CHUNKS, 0, 0, 0)
    ctx_seq = lambda s: (jnp.minimum(s // CTX_CHUNKS, CTX_B - 1), 0, 0, 0)
    const2 = lambda s: (0, 0)
    const3 = lambda s: (0, 0, 0)
    st_block = (None, N_HEADS, HEAD_DIM, HEAD_DIM)
    y_sds = jax.ShapeDtypeStruct((N_TOK, D), F32)
    s_sds = jax.ShapeDtypeStruct((CTX_B, N_HEADS, HEAD_DIM, HEAD_DIM), F32)
    return pl.pallas_call(
        _scan_kernel,
        out_shape=(y_sds, s_sds, s_sds),
        grid=(CTX_STEPS + LAT_STEPS,),
        in_specs=[
            pl.BlockSpec((CHUNK, N_RWKV_COLS), fwd),
            pl.BlockSpec((CHUNK, N_RWKV_COLS), bwd),
            pl.BlockSpec(st_block, lat_seq),
            pl.BlockSpec(st_block, lat_seq),
            pl.BlockSpec((2, 1, D), const3),
            pl.BlockSpec((2, LORA_W, D), const3),
            pl.BlockSpec((2, 1, D), const3),
            pl.BlockSpec((2, LORA_A, D), const3),
            pl.BlockSpec((1, D), const2),
            pl.BlockSpec((1, D), const2),
            pl.BlockSpec((1, D), const2),
            pl.BlockSpec((STAT_W, STAT_W), const2),
            pl.BlockSpec((1, D), const2),
            pl.BlockSpec((1, D), const2),
        ],
        out_specs=(
            pl.BlockSpec((SS_LEN, D), lambda s: (s // steps_per_ss, 0)),
            pl.BlockSpec(st_block, ctx_seq),
            pl.BlockSpec(st_block, ctx_seq),
        ),
        scratch_shapes=[
            pltpu.VMEM((2, N_HEADS // HEAD_GROUP, HEAD_GROUP * HEAD_DIM, HEAD_GROUP * HEAD_DIM), F32),
            pltpu.VMEM((SS_LEN, D), F32),
        ],
        compiler_params=_cparams(("arbitrary",)),
        name="wkv7_scan",
    )(u, u, s0f, s0b, w0, wup, a0, aup, k_k, k_a, r_k, bd, lnw, lnb)


def _convmix_kernel(z_ref, zp_ref, zn_ref, w_ref, b_ref, lnw_ref, lnb_ref, x_ref, mod_ref, y_ref, gd_ref,
                    ga_ref, gb_ref, gup_ref, wo_ref, o_ref, pad_scr, cv_scr, *, tm, halo):
    i = pl.program_id(0)
    tiles_per_lat = LAT_T // tm
    j = (i - N_CTX // tm) % tiles_per_lat
    is_ctx = i < N_CTX // tm
    keep_prev = jnp.where(jnp.logical_or(is_ctx, j == 0), 0.0, 1.0)
    keep_next = jnp.where(jnp.logical_or(is_ctx, j == tiles_per_lat - 1), 0.0, 1.0)
    pad_scr[0, 0:halo, :] = zp_ref[...] * keep_prev
    pad_scr[0, halo:halo + tm, :] = z_ref[...]
    pad_scr[0, halo + tm:halo + tm + halo, :] = zn_ref[...] * keep_next
    span = tm + 2 * halo - SUBLANES
    for m in range(1, SUBLANES):
        pad_scr[m, 0:span, :] = pad_scr[0, m:m + span, :]

    rb = 128
    lanes = 128

    def lane_block(cb, carry):
        cs = pl.ds(pl.multiple_of(cb * lanes, lanes), lanes)
        for r0 in range(0, tm, rb):
            acc = jnp.broadcast_to(b_ref[:, cs], (rb, lanes))
            for tap in range(CONV_WIDTH):
                q, m = divmod(tap + halo - CONV_HALF, SUBLANES)
                start = r0 + q * SUBLANES
                acc = acc + w_ref[tap:tap + 1, cs] * pad_scr[m, start:start + rb, cs]
            cv_scr[r0:r0 + rb, cs] = acc
        return carry

    lax.fori_loop(0, D // lanes, lane_block, 0)
    y = cv_scr[...]
    mean = jnp.mean(y, axis=-1, keepdims=True)
    yc = y - mean
    var = jnp.mean(yc * yc, axis=-1, keepdims=True)
    y = yc * lax.rsqrt(var + LN_EPS) * lnw_ref[...] + lnb_ref[...]
    y_conv = y * _sigmoid(y)

    g = _dot(_sigmoid(gd_ref[...]).astype(BF16), gup_ref[...])
    mixed = ga_ref[...] * (y_ref[...] * g) + gb_ref[...] * y_conv
    o_ref[...] = x_ref[...] + mod_ref[5:6, :] * _dot(mixed.astype(BF16), wo_ref[...])


def _convmix_call(z, w, b, lnw, lnb, x, modss, y_att, u, ga, gb, g_up, w_out, *, tm=CTX_T, halo=16):
    assert tm == CTX_T and LAT_T % tm == 0
    hb = tm // halo
    n_h = N_TOK // halo
    ss_per = SS_LEN // tm
    tok = lambda i: (i, 0)
    const2 = lambda i: (0, 0)
    gd_blk = (3 * D + 2 * LORA_W + 2 * LORA_A) // LORA_G
    kern = functools.partial(_convmix_kernel, tm=tm, halo=halo)
    return pl.pallas_call(
        kern,
        out_shape=jax.ShapeDtypeStruct((N_TOK, D), F32),
        grid=(N_TOK // tm,),
        in_specs=[
            pl.BlockSpec((tm, D), tok),
            pl.BlockSpec((halo, D), lambda i: (jnp.maximum(i * hb - 1, 0), 0)),
            pl.BlockSpec((halo, D), lambda i: (jnp.minimum((i + 1) * hb, n_h - 1), 0)),
            pl.BlockSpec((CONV_WIDTH, D), const2),
            pl.BlockSpec((1, D), const2),
            pl.BlockSpec((1, D), const2),
            pl.BlockSpec((1, D), const2),
            pl.BlockSpec((tm, D), tok),
            pl.BlockSpec((None, N_MOD, D), lambda i: (i // ss_per, 0, 0)),
            pl.BlockSpec((tm, D), tok),
            pl.BlockSpec((tm, LORA_G), lambda i: (i, gd_blk)),
            pl.BlockSpec((tm, D), tok),
            pl.BlockSpec((tm, D), tok),
            pl.BlockSpec((LORA_G, D), const2),
            pl.BlockSpec((D, D), const2),
        ],
        out_specs=pl.BlockSpec((tm, D), tok),
        scratch_shapes=[pltpu.VMEM((SUBLANES, tm + 2 * halo, D), F32), pltpu.VMEM((tm, D), F32)],
        compiler_params=_cparams(("parallel",)),
        name="conv_mixout",
    )(z, z, z, w, b, lnw, lnb, x, modss, y_att, u, ga, gb, g_up, w_out)


def _grid_pos_embed(rows):
    n_freq = D // 4
    freqs = jnp.exp(-math.log(POS_MAX_PERIOD) * jnp.arange(n_freq, dtype=F32) / n_freq)
    ang_r = jnp.arange(rows, dtype=F32)[:, None] * freqs
    ang_c = jnp.arange(GRID_W, dtype=F32)[:, None] * freqs
    emb_r = jnp.concatenate([jnp.sin(ang_r), jnp.cos(ang_r)], axis=-1)
    emb_c = jnp.concatenate([jnp.sin(ang_c), jnp.cos(ang_c)], axis=-1)
    return jnp.concatenate([jnp.repeat(emb_r, GRID_W, axis=0), jnp.tile(emb_c, (rows, 1))], axis=-1)


def kernel(x_prompt, x_sample, state_rwkv_fwd, state_rwkv_bwd, c, c_ctx, w_mod, b_mod, norm_ffn1, ffn1_w_in, ffn1_w_out, norm_mix, mix_w_in, rwkv_mu, rwkv_w0, rwkv_w_up, rwkv_a0, rwkv_a_up, rwkv_g_up, rwkv_k_k, rwkv_k_a, rwkv_r_k, rwkv_lnx_w, rwkv_lnx_b, conv_dw, conv_db, conv_ln_w, conv_ln_b, mix_w_out, norm_ffn2, ffn2_w_in, ffn2_w_out, norm_final):
    assert x_prompt.shape == (CTX_B, CTX_T, D) and x_sample.shape == (LAT_B, LAT_T, D)
    assert w_mod.shape[0] == 1, "single-layer trunk"
    row = lambda t: t.reshape(1, -1)

    cond = jnp.concatenate([c, c_ctx[None], jnp.zeros((8 - LAT_B - 1, D), F32)], axis=0)
    mod = _mod_call(cond, w_mod[0], row(b_mod[0])).reshape(8, N_MOD, D)
    ss_cond = jnp.array([LAT_B] * N_CTX_SS + list(range(LAT_B)), jnp.int32)
    modss = mod[ss_cond]

    pos = _grid_pos_embed(LAT_T // GRID_W).astype(x_sample.dtype)

    w_mix = mix_w_in[0]
    glu_off = N_RWKV_COLS
    gate_off = N_RWKV_COLS + 2 * D
    col_pad = ((0, 0), (0, N_RWKV_PAD - N_RWKV_COLS))
    w_rwkv = jnp.pad(w_mix[:, :glu_off], col_pad).astype(BF16)
    w_glu = w_mix[:, glu_off:gate_off].astype(BF16)
    w_gate = w_mix[:, gate_off:].astype(BF16)

    x1 = _ffn_first_call(x_prompt.reshape(N_CTX, D), x_sample.reshape(LAT_B * LAT_T, D), pos, modss,
                         row(norm_ffn1[0]), ffn1_w_in[0], ffn1_w_out[0])

    u = _mixin_rwkv_call(x1, modss, row(norm_mix[0]), w_rwkv, jnp.pad(row(rwkv_mu[0]), col_pad))
    z, ga, gb = _mixin_gate_call(x1, modss, row(norm_mix[0]), w_glu, w_gate)

    ii = lax.broadcasted_iota(jnp.int32, (STAT_W, STAT_W), 0) // HEAD_DIM
    jj = lax.broadcasted_iota(jnp.int32, (STAT_W, STAT_W), 1) // HEAD_DIM
    bd = jnp.where(ii == jj, 1.0, 0.0).astype(BF16)

    y_att, s_f, s_b = _scan_call(
        u, state_rwkv_fwd[:, 0], state_rwkv_bwd[:, 0],
        rwkv_w0[0].reshape(2, 1, D), rwkv_w_up[0].astype(BF16), rwkv_a0[0].reshape(2, 1, D),
        rwkv_a_up[0].astype(BF16), row(rwkv_k_k[0]), row(rwkv_k_a[0]), row(rwkv_r_k[0]), bd,
        row(rwkv_lnx_w[0]), row(rwkv_lnx_b[0]))

    x2 = _convmix_call(z, conv_dw[0], row(conv_db[0]), row(conv_ln_w[0]), row(conv_ln_b[0]),
                       x1, modss, y_att, u, ga, gb, rwkv_g_up[0].astype(BF16), mix_w_out[0].astype(BF16))

    y_ctx, y_lat = _ffn_last_call(x2, modss, row(norm_ffn2[0]), ffn2_w_in[0], ffn2_w_out[0], row(norm_final))

    st = (CTX_B, 1, N_HEADS, HEAD_DIM, HEAD_DIM)
    return (y_ctx.reshape(CTX_B, CTX_T, D), y_lat.reshape(LAT_B, LAT_T, D),
            s_f.reshape(st).astype(x_prompt.dtype), s_b.reshape(st).astype(x_prompt.dtype))
```

```python
import functools
import itertools
import math

import jax
import jax.numpy as jnp
from jax import lax
from jax.experimental import pallas as pl
from jax.experimental.pallas import tpu as pltpu

F32 = jnp.float32
BF16 = jnp.bfloat16

D = 1024
N_HEADS = 16
HEAD_DIM = 64
D_FF = 4 * D
N_MOD = 9
CONV_WIDTH = 31
CONV_HALF = CONV_WIDTH // 2
LORA_W = 64
LORA_A = 64
LORA_G = 128
N_RWKV_COLS = 3 * D + 2 * LORA_W + 2 * LORA_A + LORA_G
N_RWKV_PAD = 3584
GRID_W = 64
EPS = 1e-6
LN_EPS = 1e-5
GN_EPS = 64e-5
DECAY_SCALE = 0.606531
POS_MAX_PERIOD = 10000.0

CTX_B, CTX_T = 32, 256
LAT_B, LAT_T = 4, 2048
N_CTX = CTX_B * CTX_T
N_TOK = N_CTX + LAT_B * LAT_T
SS_LEN = 2048
N_SS = N_TOK // SS_LEN
N_CTX_SS = N_CTX // SS_LEN

CHUNK = 64
HEAD_GROUP = 2
CTX_CHUNKS = CTX_T // CHUNK
LAT_CHUNKS = LAT_T // CHUNK
CTX_STEPS = CTX_B * CTX_CHUNKS
LAT_STEPS = LAT_B * LAT_CHUNKS
STAT_W = 256
NORM_ROWS = 256
SUBLANES = 8

VMEM_LIMIT = 58 * 1024 * 1024


def _cparams(sem):
    return pltpu.CompilerParams(dimension_semantics=sem, vmem_limit_bytes=VMEM_LIMIT)


def _sigmoid(x):
    return 1.0 / (1.0 + jnp.exp(-x))


def _dot(a, b):
    return jnp.dot(a, b, preferred_element_type=F32)


def _split_bf16(x):
    hi = x.astype(BF16)
    lo = (x - hi.astype(F32)).astype(BF16)
    return hi, lo


def _head_stat(x, bd):
    groups = D // STAT_W
    stacked = jnp.concatenate([x[:, g * STAT_W:(g + 1) * STAT_W] for g in range(groups)], axis=0)
    s = _dot(stacked.astype(BF16), bd)
    rows = x.shape[0]
    return jnp.concatenate([s[g * rows:(g + 1) * rows] for g in range(groups)], axis=1)


def _mod_kernel(cond_ref, w_ref, b_ref, o_ref):
    c = cond_ref[...]
    s = (c * _sigmoid(c)).astype(BF16)
    o_ref[...] = _dot(s, w_ref[...].astype(BF16)) + b_ref[...]


def _mod_call(cond, w_mod, b_mod):
    tn = 1152
    n = N_MOD * D
    return pl.pallas_call(
        _mod_kernel,
        out_shape=jax.ShapeDtypeStruct((8, n), F32),
        grid=(n // tn,),
        in_specs=[
            pl.BlockSpec((8, D), lambda j: (0, 0)),
            pl.BlockSpec((D, tn), lambda j: (0, j)),
            pl.BlockSpec((1, tn), lambda j: (0, j)),
        ],
        out_specs=pl.BlockSpec((8, tn), lambda j: (0, j)),
        compiler_params=_cparams(("arbitrary",)),
        name="mod",
    )(cond, w_mod, b_mod)


def _rms(x, gain):
    return x * lax.rsqrt(jnp.mean(x * x, axis=-1, keepdims=True) + EPS) * gain


def _ffn_start(x, mod_ref, gain_ref, h_scr, acc_scr, mod_row):
    y = _rms(x, gain_ref[...])
    h = y * (1.0 + mod_ref[mod_row + 1:mod_row + 2, :]) + mod_ref[mod_row:mod_row + 1, :]
    h_scr[...] = h.astype(BF16)
    acc_scr[...] = jnp.zeros_like(acc_scr)


def _ffn_step(wg_ref, wu_ref, wo_ref, h_scr, acc_scr):
    h = h_scr[...]
    g = _dot(h, wg_ref[...].astype(BF16))
    u = _dot(h, wu_ref[...].astype(BF16))
    a = (g * _sigmoid(g) * u).astype(BF16)
    acc_scr[...] += _dot(a, wo_ref[...].astype(BF16))


def _ffn_first_kernel(xc_ref, xl_ref, pos_ref, mod_ref, gain_ref, wg_ref, wu_ref, wo_ref, o_ref,
                      h_scr, acc_scr, *, n_f, ctx_tiles):
    i = pl.program_id(0)
    f = pl.program_id(1)
    is_ctx = i < ctx_tiles
    sources = ((is_ctx, lambda: xc_ref[...]), (jnp.logical_not(is_ctx), lambda: xl_ref[...] + pos_ref[...]))

    for cond, x_in in sources:
        @pl.when(jnp.logical_and(f == 0, cond))
        def _():
            _ffn_start(x_in(), mod_ref, gain_ref, h_scr, acc_scr, 0)

    _ffn_step(wg_ref, wu_ref, wo_ref, h_scr, acc_scr)

    for cond, x_in in sources:
        @pl.when(jnp.logical_and(f == n_f - 1, cond))
        def _():
            o_ref[...] = x_in() + mod_ref[2:3, :] * (0.5 * acc_scr[...])


def _ffn_last_kernel(x_ref, mod_ref, gain_ref, wg_ref, wu_ref, wo_ref, gain2_ref, oc_ref, ol_ref,
                     h_scr, acc_scr, *, n_f, ctx_tiles):
    i = pl.program_id(0)
    f = pl.program_id(1)

    @pl.when(f == 0)
    def _():
        _ffn_start(x_ref[...], mod_ref, gain_ref, h_scr, acc_scr, 6)

    _ffn_step(wg_ref, wu_ref, wo_ref, h_scr, acc_scr)

    def result():
        return _rms(x_ref[...] + mod_ref[8:9, :] * (0.5 * acc_scr[...]), gain2_ref[...])

    @pl.when(jnp.logical_and(f == n_f - 1, i < ctx_tiles))
    def _():
        oc_ref[...] = result()

    @pl.when(jnp.logical_and(f == n_f - 1, i >= ctx_tiles))
    def _():
        ol_ref[...] = result()


def _ffn_specs(tm, tf):
    n_f = D_FF // tf
    ss_per = SS_LEN // tm
    return [
        pl.BlockSpec((None, N_MOD, D), lambda i, f: (i // ss_per, 0, 0)),
        pl.BlockSpec((1, D), lambda i, f: (0, 0)),
        pl.BlockSpec((D, tf), lambda i, f: (0, f)),
        pl.BlockSpec((D, tf), lambda i, f: (0, n_f + f)),
        pl.BlockSpec((tf, D), lambda i, f: (f, 0)),
    ]


def _ffn_first_call(xc, xl, pos, modss, gain, w_in, w_out, *, tm=1024, tf=512):
    n_f = D_FF // tf
    ctx_tiles = N_CTX // tm
    pos_tiles = LAT_T // tm
    kern = functools.partial(_ffn_first_kernel, n_f=n_f, ctx_tiles=ctx_tiles)
    return pl.pallas_call(
        kern,
        out_shape=jax.ShapeDtypeStruct((N_TOK, D), F32),
        grid=(N_TOK // tm, n_f),
        in_specs=[
            pl.BlockSpec((tm, D), lambda i, f: (jnp.minimum(i, ctx_tiles - 1), 0)),
            pl.BlockSpec((tm, D), lambda i, f: (jnp.maximum(i - ctx_tiles, 0), 0)),
            pl.BlockSpec((tm, D), lambda i, f: (jnp.maximum(i - ctx_tiles, 0) % pos_tiles, 0)),
        ] + _ffn_specs(tm, tf),
        out_specs=pl.BlockSpec((tm, D), lambda i, f: (i, 0)),
        scratch_shapes=[pltpu.VMEM((tm, D), BF16), pltpu.VMEM((tm, D), F32)],
        compiler_params=_cparams(("parallel", "arbitrary")),
        name="ffn_first",
    )(xc, xl, pos, modss, gain, w_in, w_in, w_out)


def _ffn_last_call(x, modss, gain, w_in, w_out, gain2, *, tm=1024, tf=512):
    n_f = D_FF // tf
    ctx_tiles = N_CTX // tm
    kern = functools.partial(_ffn_last_kernel, n_f=n_f, ctx_tiles=ctx_tiles)
    return pl.pallas_call(
        kern,
        out_shape=(jax.ShapeDtypeStruct((N_CTX, D), F32), jax.ShapeDtypeStruct((N_TOK - N_CTX, D), F32)),
        grid=(N_TOK // tm, n_f),
        in_specs=[pl.BlockSpec((tm, D), lambda i, f: (i, 0))] + _ffn_specs(tm, tf) + [
            pl.BlockSpec((1, D), lambda i, f: (0, 0))],
        out_specs=(pl.BlockSpec((tm, D), lambda i, f: (jnp.minimum(i, ctx_tiles - 1), 0)),
                   pl.BlockSpec((tm, D), lambda i, f: (jnp.maximum(i - ctx_tiles, 0), 0))),
        scratch_shapes=[pltpu.VMEM((tm, D), BF16), pltpu.VMEM((tm, D), F32)],
        compiler_params=_cparams(("parallel", "arbitrary")),
        name="ffn_last",
    )(x, modss, gain, w_in, w_in, w_out, gain2)


def _mixin_rwkv_kernel(x_ref, mod_ref, gain_ref, w_ref, mu_ref, keep_prev_ref, keep_next_ref, o_ref, h_scr):
    j = pl.program_id(1)

    @pl.when(j == 0)
    def _():
        y = _rms(x_ref[...], gain_ref[...])
        h_scr[...] = (y * (1.0 + mod_ref[4:5, :]) + mod_ref[3:4, :]).astype(BF16)

    u = _dot(h_scr[...], w_ref[...])
    rows, cols = u.shape
    keep_prev = keep_prev_ref[...]
    keep_next = keep_next_ref[...]
    lanes = keep_prev.shape[1]
    for l in range(cols // lanes):
        ls = slice(l * lanes, (l + 1) * lanes)
        ul = u[:, ls]
        neighbours = pltpu.roll(ul, 1, 0) * keep_prev + pltpu.roll(ul, rows - 1, 0) * keep_next
        mu = mu_ref[:, ls]
        o_ref[:, ls] = ul * (1.0 - mu) + neighbours * (0.5 * mu)


def _mixin_rwkv_call(x, modss, gain, w, mu, *, tn=512, lanes=128):
    tm = SS_LEN
    pos = lax.broadcasted_iota(jnp.int32, (2, tm, lanes), 1)
    seq_len = jnp.array([CTX_T, LAT_T], jnp.int32)[:, None, None]
    pos = pos & (seq_len - 1)
    keep_prev = jnp.where(pos == 0, 0.0, 1.0)
    keep_next = jnp.where(pos == seq_len - 1, 0.0, 1.0)
    kind = lambda i, j: (jnp.where(i < N_CTX_SS, 0, 1), 0, 0)
    return pl.pallas_call(
        _mixin_rwkv_kernel,
        out_shape=jax.ShapeDtypeStruct((N_TOK, N_RWKV_PAD), F32),
        grid=(N_TOK // tm, N_RWKV_PAD // tn),
        in_specs=[
            pl.BlockSpec((tm, D), lambda i, j: (i, 0)),
            pl.BlockSpec((None, N_MOD, D), lambda i, j: (i, 0, 0)),
            pl.BlockSpec((1, D), lambda i, j: (0, 0)),
            pl.BlockSpec((D, tn), lambda i, j: (0, j)),
            pl.BlockSpec((1, tn), lambda i, j: (0, j)),
            pl.BlockSpec((None, tm, lanes), kind),
            pl.BlockSpec((None, tm, lanes), kind),
        ],
        out_specs=pl.BlockSpec((tm, tn), lambda i, j: (i, j)),
        scratch_shapes=[pltpu.VMEM((tm, D), BF16)],
        compiler_params=_cparams(("parallel", "arbitrary")),
        name="mixin_rwkv",
    )(x, modss, gain, w, mu, keep_prev, keep_next)


def _mixin_gate_kernel(x_ref, mod_ref, gain_ref, wa_ref, wb_ref, wga_ref, wgb_ref, z_ref, ga_ref, gb_ref,
                       h_scr):
    j = pl.program_id(1)

    @pl.when(j == 0)
    def _():
        y = _rms(x_ref[...], gain_ref[...])
        h_scr[...] = (y * (1.0 + mod_ref[4:5, :]) + mod_ref[3:4, :]).astype(BF16)

    h = h_scr[...]
    z_ref[...] = _dot(h, wa_ref[...]) * _sigmoid(_dot(h, wb_ref[...]))
    ga_ref[...] = _sigmoid(_dot(h, wga_ref[...]))
    gb_ref[...] = _sigmoid(_dot(h, wgb_ref[...]))


def _mixin_gate_call(x, modss, gain, w_glu, w_gate, *, tm=1024, tn=512):
    n_j = D // tn
    ss_per = SS_LEN // tm
    sds = jax.ShapeDtypeStruct((N_TOK, D), F32)
    lo = lambda i, j: (0, j)
    hi = lambda i, j: (0, n_j + j)
    out = pl.BlockSpec((tm, tn), lambda i, j: (i, j))
    return pl.pallas_call(
        _mixin_gate_kernel,
        out_shape=(sds, sds, sds),
        grid=(N_TOK // tm, n_j),
        in_specs=[
            pl.BlockSpec((tm, D), lambda i, j: (i, 0)),
            pl.BlockSpec((None, N_MOD, D), lambda i, j: (i // ss_per, 0, 0)),
            pl.BlockSpec((1, D), lambda i, j: (0, 0)),
            pl.BlockSpec((D, tn), lo),
            pl.BlockSpec((D, tn), hi),
            pl.BlockSpec((D, tn), lo),
            pl.BlockSpec((D, tn), hi),
        ],
        out_specs=(out, out, out),
        scratch_shapes=[pltpu.VMEM((tm, D), BF16)],
        compiler_params=_cparams(("parallel", "arbitrary")),
        name="mixin_gate",
    )(x, modss, gain, w_glu, w_glu, w_gate, w_gate)


def _scan_prep(x_ref, d, reverse, prm, out):
    (w0_ref, wup_ref, a0_ref, aup_ref, kk_ref, ka_ref, rk_ref, bd_ref) = prm
    c = CHUNK
    wd = x_ref[:, 3 * D + d * LORA_W:3 * D + (d + 1) * LORA_W]
    ad_off = 3 * D + 2 * LORA_W
    ad = x_ref[:, ad_off + d * LORA_A:ad_off + (d + 1) * LORA_A]
    logw = -DECAY_SCALE * _sigmoid(w0_ref[d] + _dot(jnp.tanh(wd).astype(BF16), wup_ref[d]))
    a = _sigmoid(a0_ref[d] + _dot(ad.astype(BF16), aup_ref[d]))
    yield

    k = x_ref[:, D:2 * D]
    kk = k * kk_ref[...]
    kk = kk * lax.rsqrt(_head_stat(kk * kk, bd_ref[...]) + 1e-12)
    kd = k * (1.0 + (a - 1.0) * ka_ref[...])
    b = kk * a
    yield

    v = x_ref[:, 2 * D:3 * D]
    out["v"] = v
    out["bonus"] = _head_stat(x_ref[:, 0:D] * kd * rk_ref[...], bd_ref[...]) * v
    ti = lax.broadcasted_iota(jnp.int32, (c, c), 0)
    si = lax.broadcasted_iota(jnp.int32, (c, c), 1)
    tri = jnp.where((ti <= si) if reverse else (ti >= si), 1.0, 0.0).astype(BF16)
    lw_hi, lw_lo = _split_bf16(logw)
    cum = _dot(tri, lw_hi) + _dot(tri, lw_lo)
    yield

    tot = cum[0:1, :] if reverse else cum[c - 1:c, :]
    p_inv = jnp.exp(-cum)
    p_tot = jnp.exp(tot)
    r_b = b * p_inv
    r_k = kd * p_inv
    out.update(
        l_kk=(kk * jnp.exp(cum - logw)).astype(BF16), l_r=(x_ref[:, 0:D] * jnp.exp(cum)).astype(BF16),
        r_b=r_b, r_k=r_k, s_b=(r_b * p_tot).astype(BF16), s_k=(r_k * p_tot).astype(BF16), p_tot=p_tot)
    yield


def _group_norm(y, bd, lnw, lnb):
    yc = y - _head_stat(y, bd) * (1.0 / HEAD_DIM)
    var = _head_stat(yc * yc, bd) * (1.0 / HEAD_DIM)
    return yc * lax.rsqrt(var + GN_EPS) * lnw + lnb


_BANK_F32 = ("r_b", "r_k", "v", "bonus")
_BANK_BF16 = ("l_kk", "l_r", "s_b", "s_k")
_BANK_NAMES = _BANK_F32 + _BANK_BF16 + ("p_tot",)
_BANK_EARLY = ("l_kk", "l_r", "r_b", "r_k")


def _scan_kernel(x0f_ref, x0b_ref, xnf_ref, xnb_ref, s0f_ref, s0b_ref, w0_ref, wup_ref, a0_ref, aup_ref,
                 kk_ref, ka_ref, rk_ref, bd_ref, lnw_ref, lnb_ref, y_ref, sf_ref, sb_ref, s_scr, bonus_scr,
                 *bank_refs):
    s = pl.program_id(0)
    is_ctx = s < CTX_STEPS
    n_chunks = jnp.where(is_ctx, CTX_CHUNKS, LAT_CHUNKS)
    t_step = jnp.where(is_ctx, s, s - CTX_STEPS)
    cidx = t_step & (n_chunks - 1)
    seq_in_ss = jnp.where(is_ctx, (t_step >> (CTX_CHUNKS.bit_length() - 1)) & (SS_LEN // CTX_T - 1), 0)
    row_f = pl.multiple_of((seq_in_ss * n_chunks + cidx) * CHUNK, CHUNK)
    row_b = pl.multiple_of((seq_in_ss * n_chunks + n_chunks - 1 - cidx) * CHUNK, CHUNK)
    second_visit = cidx >= (n_chunks >> 1)
    prm = (w0_ref, wup_ref, a0_ref, aup_ref, kk_ref, ka_ref, rk_ref, bd_ref)
    c, g, gw = CHUNK, HEAD_GROUP, HEAD_GROUP * HEAD_DIM
    n_groups = N_HEADS // g
    s0_refs = (s0f_ref, s0b_ref)

    @pl.when(jnp.logical_and(cidx == 0, is_ctx))
    def _():
        s_scr[...] = jnp.zeros_like(s_scr)

    @pl.when(jnp.logical_and(cidx == 0, jnp.logical_not(is_ctx)))
    def _():
        for d in range(2):
            for gi in range(n_groups):
                for l in range(g):
                    pieces = [jnp.zeros((HEAD_DIM, HEAD_DIM), F32)] * g
                    pieces[l] = s0_refs[d][gi * g + l]
                    s_scr[d, gi, l * HEAD_DIM:(l + 1) * HEAD_DIM, :] = jnp.concatenate(pieces, axis=1)

    bank = dict(zip(_BANK_NAMES, bank_refs))

    def store(nxt, names):
        for d in range(2):
            for name in names:
                bank[name][d] = nxt[d][name]

    @pl.when(s == 0)
    def _():
        first = ({}, {})
        for _ in itertools.chain(_scan_prep(x0f_ref, 0, False, prm, first[0]),
                                 _scan_prep(x0b_ref, 1, True, prm, first[1])):
            pass
        store(first, _BANK_NAMES)

    nxt = ({}, {})
    prep_next = itertools.chain(_scan_prep(xnf_ref, 0, False, prm, nxt[0]), _scan_prep(xnb_ref, 1, True, prm, nxt[1]))
    bonus_now = [bank["bonus"][d] for d in range(2)]

    row = lax.broadcasted_iota(jnp.int32, (c, gw), 0)
    lane = lax.broadcasted_iota(jnp.int32, (c, gw), 1)
    col = lane & (HEAD_DIM - 1)
    head_of_lane = lane >> 6
    eye = jnp.where(row == col, 1.0, 0.0)
    strict = (row > col, row < col)
    incl = (row >= col, row <= col)
    srow = lax.broadcasted_iota(jnp.int32, (gw, gw), 0) >> 6
    slane = lax.broadcasted_iota(jnp.int32, (gw, gw), 1) >> 6
    state_mask = srow == slane
    nt = (((1,), (1,)), ((), ()))
    tn = (((0,), (0,)), ((), ()))

    def blockdiag(x):
        return jnp.concatenate([jnp.where(head_of_lane == l, x, 0.0) for l in range(g)], axis=0).astype(BF16)

    def same_block(n):
        shift = n.bit_length() - 1
        return (row >> shift) == (col >> shift)

    items = [(d, gi) for d in range(2) for gi in range(n_groups)]
    cols = lambda gi: slice(gi * gw, (gi + 1) * gw)
    get = lambda name, d, gi: bank[name][d, :, cols(gi)]

    s_old = [s_scr[d, gi] for d, gi in items]
    lhs = [jnp.concatenate([get("l_kk", d, gi), get("l_r", d, gi)], axis=0) for d, gi in items]
    rt = [jnp.concatenate([blockdiag(get("r_b", d, gi)), blockdiag(get("r_k", d, gi))], axis=0)
          for d, gi in items]
    amat = [lax.dot_general(lhs[i], rt[i], nt, preferred_element_type=F32) for i in range(len(items))]
    ls = [lax.dot_general(lhs[i], s_old[i].astype(BF16), nt, preferred_element_type=F32) for i in range(len(items))]
    a_ab = [jnp.where(strict[d], amat[i][:c, :gw], 0.0) for i, (d, gi) in enumerate(items)]
    a_ak = [jnp.where(strict[d], amat[i][:c, gw:], 0.0).astype(BF16) for i, (d, gi) in enumerate(items)]
    a_r = [jnp.concatenate([jnp.where(incl[d], amat[i][c:, :gw], 0.0), jnp.where(incl[d], amat[i][c:, gw:], 0.0)],
                           axis=1).astype(BF16) for i, (d, gi) in enumerate(items)]
    bd_v = [blockdiag(get("v", d, gi)) for d, gi in items]
    rhs = [-(ls[i][:c] + _dot(a_ak[i], bd_v[i])) for i in range(len(items))]
    next(prep_next, None)

    tinv = [eye - jnp.where(same_block(2), a, 0.0) for a in a_ab]
    n = 2
    while n < c:
        lower = jnp.logical_and(same_block(2 * n), jnp.logical_not(same_block(n)))
        off = [jnp.where(lower, a, 0.0).astype(BF16) for a in a_ab]
        w = [_dot(off[i], blockdiag(tinv[i])) for i in range(len(items))]
        tinv = [tinv[i] - _dot(tinv[i].astype(BF16), blockdiag(w[i])) for i in range(len(items))]
        next(prep_next, None)
        n *= 2

    u = [_dot(tinv[i].astype(BF16), blockdiag(rhs[i])) for i in range(len(items))]
    next(prep_next, None)
    y = [ls[i][c:] + _dot(a_r[i], jnp.concatenate([blockdiag(u[i]), bd_v[i]], axis=0)) for i in range(len(items))]
    for _ in prep_next:
        pass
    store(nxt, _BANK_EARLY)
    for i, (d, gi) in enumerate(items):
        uv = jnp.concatenate([u[i].astype(BF16), get("v", d, gi).astype(BF16)], axis=0)
        sbk = jnp.concatenate([get("s_b", d, gi), get("s_k", d, gi)], axis=0)
        upd = lax.dot_general(uv, sbk, tn, preferred_element_type=F32)
        s_scr[d, gi] = s_old[i] * get("p_tot", d, gi) + jnp.where(state_mask, upd, 0.0)
    store(nxt, [name for name in _BANK_NAMES if name not in _BANK_EARLY])

    visits = ((row_f, jnp.concatenate(y[:n_groups], axis=1), bonus_now[0]),
              (row_b, jnp.concatenate(y[n_groups:], axis=1), bonus_now[1]))

    @pl.when(jnp.logical_not(second_visit))
    def _():
        for rows, y_d, bonus_d in visits:
            y_ref[pl.ds(rows, c), :] = y_d
            bonus_scr[pl.ds(rows, c), :] = bonus_d

    @pl.when(second_visit)
    def _():
        for rows, y_d, bonus_d in visits:
            y_ref[pl.ds(rows, c), :] = y_ref[pl.ds(rows, c), :] + y_d
            bonus_scr[pl.ds(rows, c), :] = bonus_scr[pl.ds(rows, c), :] + bonus_d

    @pl.when(s % (SS_LEN // CHUNK) == SS_LEN // CHUNK - 1)
    def _():
        def norm_block(i, carry):
            rows = pl.ds(pl.multiple_of(i * NORM_ROWS, NORM_ROWS), NORM_ROWS)
            y_ref[rows, :] = (_group_norm(y_ref[rows, :], bd_ref[...], lnw_ref[...], lnb_ref[...])
                              + bonus_scr[rows, :])
            return carry

        lax.fori_loop(0, SS_LEN // NORM_ROWS, norm_block, 0)

    @pl.when(jnp.logical_and(cidx == n_chunks - 1, is_ctx))
    def _():
        for d, out_ref in enumerate((sf_ref, sb_ref)):
            for h in range(N_HEADS):
                gi, l = divmod(h, g)
                hs = slice(l * HEAD_DIM, (l + 1) * HEAD_DIM)
                out_ref[h] = s_scr[d, gi, hs, hs]


def _scan_row_block(s, reverse):
    ctx = s < CTX_STEPS
    n_chunks = jnp.where(ctx, CTX_CHUNKS, LAT_CHUNKS)
    t = jnp.where(ctx, s, s - CTX_STEPS)
    cidx = t & (n_chunks - 1)
    if reverse:
        cidx = n_chunks - 1 - cidx
    base = jnp.where(ctx, 0, CTX_STEPS)
    return base + (t - (t & (n_chunks - 1))) + cidx


def _scan_call(u, s0f, s0b, w0, wup, a0, aup, k_k, k_a, r_k, bd, lnw, lnb):
    assert CTX_STEPS // N_CTX_SS == LAT_CHUNKS, "context and latent super sequences take the same number of steps"
    steps_per_ss = LAT_CHUNKS
    n_steps = CTX_STEPS + LAT_STEPS
    first_f = lambda s: (_scan_row_block(s * 0, False), 0)
    first_b = lambda s: (_scan_row_block(s * 0, True), 0)
    next_f = lambda s: (_scan_row_block(jnp.minimum(s + 1, n_steps - 1), False), 0)
    next_b = lambda s: (_scan_row_block(jnp.minimum(s + 1, n_steps - 1), True), 0)
    lat_seq = lambda s: (jnp.maximum(s - CTX_STEPS, 0) // LAT_CHUNKS, 0, 0, 0)
    ctx_seq = lambda s: (jnp.minimum(s // CTX_CHUNKS, CTX_B - 1), 0, 0, 0)
    const2 = lambda s: (0, 0)
    const3 = lambda s: (0, 0, 0)
    st_block = (None, N_HEADS, HEAD_DIM, HEAD_DIM)
    y_sds = jax.ShapeDtypeStruct((N_TOK, D), F32)
    s_sds = jax.ShapeDtypeStruct((CTX_B, N_HEADS, HEAD_DIM, HEAD_DIM), F32)
    return pl.pallas_call(
        _scan_kernel,
        out_shape=(y_sds, s_sds, s_sds),
        grid=(n_steps,),
        in_specs=[
            pl.BlockSpec((CHUNK, N_RWKV_COLS), first_f),
            pl.BlockSpec((CHUNK, N_RWKV_COLS), first_b),
            pl.BlockSpec((CHUNK, N_RWKV_COLS), next_f),
            pl.BlockSpec((CHUNK, N_RWKV_COLS), next_b),
            pl.BlockSpec(st_block, lat_seq),
            pl.BlockSpec(st_block, lat_seq),
            pl.BlockSpec((2, 1, D), const3),
            pl.BlockSpec((2, LORA_W, D), const3),
            pl.BlockSpec((2, 1, D), const3),
            pl.BlockSpec((2, LORA_A, D), const3),
            pl.BlockSpec((1, D), const2),
            pl.BlockSpec((1, D), const2),
            pl.BlockSpec((1, D), const2),
            pl.BlockSpec((STAT_W, STAT_W), const2),
            pl.BlockSpec((1, D), const2),
            pl.BlockSpec((1, D), const2),
        ],
        out_specs=(
            pl.BlockSpec((SS_LEN, D), lambda s: (s // steps_per_ss, 0)),
            pl.BlockSpec(st_block, ctx_seq),
            pl.BlockSpec(st_block, ctx_seq),
        ),
        scratch_shapes=[
            pltpu.VMEM((2, N_HEADS // HEAD_GROUP, HEAD_GROUP * HEAD_DIM, HEAD_GROUP * HEAD_DIM), F32),
            pltpu.VMEM((SS_LEN, D), F32),
        ] + [pltpu.VMEM((2, CHUNK, D), F32)] * len(_BANK_F32) + [pltpu.VMEM((2, CHUNK, D), BF16)] * len(_BANK_BF16)
        + [pltpu.VMEM((2, 1, D), F32)],
        compiler_params=_cparams(("arbitrary",)),
        name="wkv7_scan",
    )(u, u, u, u, s0f, s0b, w0, wup, a0, aup, k_k, k_a, r_k, bd, lnw, lnb)


def _convmix_kernel(z_ref, zp_ref, zn_ref, w_ref, b_ref, lnw_ref, lnb_ref, x_ref, mod_ref, y_ref, gd_ref,
                    ga_ref, gb_ref, gup_ref, wo_ref, o_ref, pad_scr, cv_scr, *, tm, halo):
    i = pl.program_id(0)
    tiles_per_lat = LAT_T // tm
    j = (i - N_CTX // tm) % tiles_per_lat
    is_ctx = i < N_CTX // tm
    keep_prev = jnp.where(jnp.logical_or(is_ctx, j == 0), 0.0, 1.0)
    keep_next = jnp.where(jnp.logical_or(is_ctx, j == tiles_per_lat - 1), 0.0, 1.0)
    pad_scr[0, 0:halo, :] = zp_ref[...] * keep_prev
    pad_scr[0, halo:halo + tm, :] = z_ref[...]
    pad_scr[0, halo + tm:halo + tm + halo, :] = zn_ref[...] * keep_next
    span = tm + 2 * halo - SUBLANES
    for m in range(1, SUBLANES):
        pad_scr[m, 0:span, :] = pad_scr[0, m:m + span, :]

    rb = 128
    lanes = 128

    def lane_block(cb, carry):
        cs = pl.ds(pl.multiple_of(cb * lanes, lanes), lanes)
        for r0 in range(0, tm, rb):
            acc = jnp.broadcast_to(b_ref[:, cs], (rb, lanes))
            for tap in range(CONV_WIDTH):
                q, m = divmod(tap + halo - CONV_HALF, SUBLANES)
                start = r0 + q * SUBLANES
                acc = acc + w_ref[tap:tap + 1, cs] * pad_scr[m, start:start + rb, cs]
            cv_scr[r0:r0 + rb, cs] = acc
        return carry

    lax.fori_loop(0, D // lanes, lane_block, 0)
    y = cv_scr[...]
    mean = jnp.mean(y, axis=-1, keepdims=True)
    yc = y - mean
    var = jnp.mean(yc * yc, axis=-1, keepdims=True)
    y = yc * lax.rsqrt(var + LN_EPS) * lnw_ref[...] + lnb_ref[...]
    y_conv = y * _sigmoid(y)

    g = _dot(_sigmoid(gd_ref[...]).astype(BF16), gup_ref[...])
    mixed = ga_ref[...] * (y_ref[...] * g) + gb_ref[...] * y_conv
    o_ref[...] = x_ref[...] + mod_ref[5:6, :] * _dot(mixed.astype(BF16), wo_ref[...])


def _convmix_call(z, w, b, lnw, lnb, x, modss, y_att, u, ga, gb, g_up, w_out, *, tm=CTX_T, halo=16):
    assert tm == CTX_T and LAT_T % tm == 0
    hb = tm // halo
    n_h = N_TOK // halo
    ss_per = SS_LEN // tm
    tok = lambda i: (i, 0)
    const2 = lambda i: (0, 0)
    gd_blk = (3 * D + 2 * LORA_W + 2 * LORA_A) // LORA_G
    kern = functools.partial(_convmix_kernel, tm=tm, halo=halo)
    return pl.pallas_call(
        kern,
        out_shape=jax.ShapeDtypeStruct((N_TOK, D), F32),
        grid=(N_TOK // tm,),
        in_specs=[
            pl.BlockSpec((tm, D), tok),
            pl.BlockSpec((halo, D), lambda i: (jnp.maximum(i * hb - 1, 0), 0)),
            pl.BlockSpec((halo, D), lambda i: (jnp.minimum((i + 1) * hb, n_h - 1), 0)),
            pl.BlockSpec((CONV_WIDTH, D), const2),
            pl.BlockSpec((1, D), const2),
            pl.BlockSpec((1, D), const2),
            pl.BlockSpec((1, D), const2),
            pl.BlockSpec((tm, D), tok),
            pl.BlockSpec((None, N_MOD, D), lambda i: (i // ss_per, 0, 0)),
            pl.BlockSpec((tm, D), tok),
            pl.BlockSpec((tm, LORA_G), lambda i: (i, gd_blk)),
            pl.BlockSpec((tm, D), tok),
            pl.BlockSpec((tm, D), tok),
            pl.BlockSpec((LORA_G, D), const2),
            pl.BlockSpec((D, D), const2),
        ],
        out_specs=pl.BlockSpec((tm, D), tok),
        scratch_shapes=[pltpu.VMEM((SUBLANES, tm + 2 * halo, D), F32), pltpu.VMEM((tm, D), F32)],
        compiler_params=_cparams(("parallel",)),
        name="conv_mixout",
    )(z, z, z, w, b, lnw, lnb, x, modss, y_att, u, ga, gb, g_up, w_out)


def _grid_pos_embed(rows):
    n_freq = D // 4
    freqs = jnp.exp(-math.log(POS_MAX_PERIOD) * jnp.arange(n_freq, dtype=F32) / n_freq)
    ang_r = jnp.arange(rows, dtype=F32)[:, None] * freqs
    ang_c = jnp.arange(GRID_W, dtype=F32)[:, None] * freqs
    emb_r = jnp.concatenate([jnp.sin(ang_r), jnp.cos(ang_r)], axis=-1)
    emb_c = jnp.concatenate([jnp.sin(ang_c), jnp.cos(ang_c)], axis=-1)
    return jnp.concatenate([jnp.repeat(emb_r, GRID_W, axis=0), jnp.tile(emb_c, (rows, 1))], axis=-1)


def kernel(x_prompt, x_sample, state_rwkv_fwd, state_rwkv_bwd, c, c_ctx, w_mod, b_mod, norm_ffn1, ffn1_w_in, ffn1_w_out, norm_mix, mix_w_in, rwkv_mu, rwkv_w0, rwkv_w_up, rwkv_a0, rwkv_a_up, rwkv_g_up, rwkv_k_k, rwkv_k_a, rwkv_r_k, rwkv_lnx_w, rwkv_lnx_b, conv_dw, conv_db, conv_ln_w, conv_ln_b, mix_w_out, norm_ffn2, ffn2_w_in, ffn2_w_out, norm_final):
    assert x_prompt.shape == (CTX_B, CTX_T, D) and x_sample.shape == (LAT_B, LAT_T, D)
    assert w_mod.shape[0] == 1, "single-layer trunk"
    row = lambda t: t.reshape(1, -1)

    cond = jnp.concatenate([c, c_ctx[None], jnp.zeros((8 - LAT_B - 1, D), F32)], axis=0)
    mod = _mod_call(cond, w_mod[0], row(b_mod[0])).reshape(8, N_MOD, D)
    ss_cond = jnp.array([LAT_B] * N_CTX_SS + list(range(LAT_B)), jnp.int32)
    modss = mod[ss_cond]

    pos = _grid_pos_embed(LAT_T // GRID_W).astype(x_sample.dtype)

    w_mix = mix_w_in[0]
    glu_off = N_RWKV_COLS
    gate_off = N_RWKV_COLS + 2 * D
    col_pad = ((0, 0), (0, N_RWKV_PAD - N_RWKV_COLS))
    w_rwkv = jnp.pad(w_mix[:, :glu_off], col_pad).astype(BF16)
    w_glu = w_mix[:, glu_off:gate_off].astype(BF16)
    w_gate = w_mix[:, gate_off:].astype(BF16)

    x1 = _ffn_first_call(x_prompt.reshape(N_CTX, D), x_sample.reshape(LAT_B * LAT_T, D), pos, modss,
                         row(norm_ffn1[0]), ffn1_w_in[0], ffn1_w_out[0])

    u = _mixin_rwkv_call(x1, modss, row(norm_mix[0]), w_rwkv, jnp.pad(row(rwkv_mu[0]), col_pad))
    z, ga, gb = _mixin_gate_call(x1, modss, row(norm_mix[0]), w_glu, w_gate)

    ii = lax.broadcasted_iota(jnp.int32, (STAT_W, STAT_W), 0) // HEAD_DIM
    jj = lax.broadcasted_iota(jnp.int32, (STAT_W, STAT_W), 1) // HEAD_DIM
    bd = jnp.where(ii == jj, 1.0, 0.0).astype(BF16)

    y_att, s_f, s_b = _scan_call(
        u, state_rwkv_fwd[:, 0], state_rwkv_bwd[:, 0],
        rwkv_w0[0].reshape(2, 1, D), rwkv_w_up[0].astype(BF16), rwkv_a0[0].reshape(2, 1, D),
        rwkv_a_up[0].astype(BF16), row(rwkv_k_k[0]), row(rwkv_k_a[0]), row(rwkv_r_k[0]), bd,
        row(rwkv_lnx_w[0]), row(rwkv_lnx_b[0]))

    x2 = _convmix_call(z, conv_dw[0], row(conv_db[0]), row(conv_ln_w[0]), row(conv_ln_b[0]),
                       x1, modss, y_att, u, ga, gb, rwkv_g_up[0].astype(BF16), mix_w_out[0].astype(BF16))

    y_ctx, y_lat = _ffn_last_call(x2, modss, row(norm_ffn2[0]), ffn2_w_in[0], ffn2_w_out[0], row(norm_final))

    st = (CTX_B, 1, N_HEADS, HEAD_DIM, HEAD_DIM)
    return (y_ctx.reshape(CTX_B, CTX_T, D), y_lat.reshape(LAT_B, LAT_T, D),
            s_f.reshape(st).astype(x_prompt.dtype), s_b.reshape(st).astype(x_prompt.dtype))
```

```python
import functools
import math

import jax
import jax.numpy as jnp
from jax import lax
from jax.experimental import pallas as pl
from jax.experimental.pallas import tpu as pltpu

F32 = jnp.float32
BF16 = jnp.bfloat16

D = 1024
N_HEADS = 16
HEAD_DIM = 64
D_FF = 4 * D
N_MOD = 9
CONV_WIDTH = 31
CONV_HALF = CONV_WIDTH // 2
LORA_W = 64
LORA_A = 64
LORA_G = 128
N_RWKV_COLS = 3 * D + 2 * LORA_W + 2 * LORA_A + LORA_G
N_RWKV_PAD = 3584
GRID_W = 64
EPS = 1e-6
LN_EPS = 1e-5
GN_EPS = 64e-5
DECAY_SCALE = 0.606531
POS_MAX_PERIOD = 10000.0

CTX_B, CTX_T = 32, 256
LAT_B, LAT_T = 4, 2048
N_CTX = CTX_B * CTX_T
N_TOK = N_CTX + LAT_B * LAT_T
SS_LEN = 2048
N_CTX_SS = N_CTX // SS_LEN

CHUNK = 64
HEAD_GROUP = 2
CTX_CHUNKS = CTX_T // CHUNK
LAT_CHUNKS = LAT_T // CHUNK
CTX_STEPS = CTX_B * CTX_CHUNKS
LAT_STEPS = LAT_B * LAT_CHUNKS
STAT_W = 256
NORM_ROWS = 256
SUBLANES = 8

VMEM_LIMIT = 58 * 1024 * 1024


def _cparams(sem):
    return pltpu.CompilerParams(dimension_semantics=sem, vmem_limit_bytes=VMEM_LIMIT)


def _sigmoid(x):
    return 1.0 / (1.0 + jnp.exp(-x))


def _dot(a, b):
    return jnp.dot(a, b, preferred_element_type=F32)


def _split_bf16(x):
    hi = x.astype(BF16)
    lo = (x - hi.astype(F32)).astype(BF16)
    return hi, lo


def _head_stat(x, bd):
    groups = D // STAT_W
    stacked = jnp.concatenate([x[:, g * STAT_W:(g + 1) * STAT_W] for g in range(groups)], axis=0)
    s = _dot(stacked.astype(BF16), bd)
    rows = x.shape[0]
    return jnp.concatenate([s[g * rows:(g + 1) * rows] for g in range(groups)], axis=1)


def _mod_kernel(cond_ref, w_ref, b_ref, o_ref):
    c = cond_ref[...]
    s = (c * _sigmoid(c)).astype(BF16)
    o_ref[...] = _dot(s, w_ref[...].astype(BF16)) + b_ref[...]


def _mod_call(cond, w_mod, b_mod):
    tn = 1152
    n = N_MOD * D
    return pl.pallas_call(
        _mod_kernel,
        out_shape=jax.ShapeDtypeStruct((8, n), F32),
        grid=(n // tn,),
        in_specs=[
            pl.BlockSpec((8, D), lambda j: (0, 0)),
            pl.BlockSpec((D, tn), lambda j: (0, j)),
            pl.BlockSpec((1, tn), lambda j: (0, j)),
        ],
        out_specs=pl.BlockSpec((8, tn), lambda j: (0, j)),
        compiler_params=_cparams(("arbitrary",)),
        name="mod",
    )(cond, w_mod, b_mod)


def _rms(x, gain):
    return x * lax.rsqrt(jnp.mean(x * x, axis=-1, keepdims=True) + EPS) * gain


def _ffn_start(x, mod_ref, gain_ref, h_scr, acc_scr, mod_row):
    y = _rms(x, gain_ref[...])
    h = y * (1.0 + mod_ref[mod_row + 1:mod_row + 2, :]) + mod_ref[mod_row:mod_row + 1, :]
    h_scr[...] = h.astype(BF16)
    acc_scr[...] = jnp.zeros_like(acc_scr)


def _ffn_step(wg_ref, wu_ref, wo_ref, h_scr, acc_scr):
    h = h_scr[...]
    g = _dot(h, wg_ref[...].astype(BF16))
    u = _dot(h, wu_ref[...].astype(BF16))
    a = (g * _sigmoid(g) * u).astype(BF16)
    acc_scr[...] += _dot(a, wo_ref[...].astype(BF16))


def _ffn_first_kernel(xc_ref, xl_ref, pos_ref, mod_ref, gain_ref, wg_ref, wu_ref, wo_ref, o_ref,
                      h_scr, acc_scr, *, n_f, ctx_tiles):
    i = pl.program_id(0)
    f = pl.program_id(1)
    is_ctx = i < ctx_tiles
    sources = ((is_ctx, lambda: xc_ref[...]), (jnp.logical_not(is_ctx), lambda: xl_ref[...] + pos_ref[...]))

    for cond, x_in in sources:
        @pl.when(jnp.logical_and(f == 0, cond))
        def _():
            _ffn_start(x_in(), mod_ref, gain_ref, h_scr, acc_scr, 0)

    _ffn_step(wg_ref, wu_ref, wo_ref, h_scr, acc_scr)

    for cond, x_in in sources:
        @pl.when(jnp.logical_and(f == n_f - 1, cond))
        def _():
            o_ref[...] = x_in() + mod_ref[2:3, :] * (0.5 * acc_scr[...])


def _ffn_last_kernel(x_ref, mod_ref, gain_ref, wg_ref, wu_ref, wo_ref, gain2_ref, oc_ref, ol_ref,
                     h_scr, acc_scr, *, n_f, ctx_tiles):
    i = pl.program_id(0)
    f = pl.program_id(1)

    @pl.when(f == 0)
    def _():
        _ffn_start(x_ref[...], mod_ref, gain_ref, h_scr, acc_scr, 6)

    _ffn_step(wg_ref, wu_ref, wo_ref, h_scr, acc_scr)

    def result():
        return _rms(x_ref[...] + mod_ref[8:9, :] * (0.5 * acc_scr[...]), gain2_ref[...])

    @pl.when(jnp.logical_and(f == n_f - 1, i < ctx_tiles))
    def _():
        oc_ref[...] = result()

    @pl.when(jnp.logical_and(f == n_f - 1, i >= ctx_tiles))
    def _():
        ol_ref[...] = result()


def _ffn_specs(tm, tf):
    n_f = D_FF // tf
    ss_per = SS_LEN // tm
    return [
        pl.BlockSpec((None, N_MOD, D), lambda i, f: (i // ss_per, 0, 0)),
        pl.BlockSpec((1, D), lambda i, f: (0, 0)),
        pl.BlockSpec((D, tf), lambda i, f: (0, f)),
        pl.BlockSpec((D, tf), lambda i, f: (0, n_f + f)),
        pl.BlockSpec((tf, D), lambda i, f: (f, 0)),
    ]


def _ffn_first_call(xc, xl, pos, modss, gain, w_in, w_out, *, tm=1024, tf=512):
    n_f = D_FF // tf
    ctx_tiles = N_CTX // tm
    pos_tiles = LAT_T // tm
    kern = functools.partial(_ffn_first_kernel, n_f=n_f, ctx_tiles=ctx_tiles)
    return pl.pallas_call(
        kern,
        out_shape=jax.ShapeDtypeStruct((N_TOK, D), F32),
        grid=(N_TOK // tm, n_f),
        in_specs=[
            pl.BlockSpec((tm, D), lambda i, f: (jnp.minimum(i, ctx_tiles - 1), 0)),
            pl.BlockSpec((tm, D), lambda i, f: (jnp.maximum(i - ctx_tiles, 0), 0)),
            pl.BlockSpec((tm, D), lambda i, f: (jnp.maximum(i - ctx_tiles, 0) % pos_tiles, 0)),
        ] + _ffn_specs(tm, tf),
        out_specs=pl.BlockSpec((tm, D), lambda i, f: (i, 0)),
        scratch_shapes=[pltpu.VMEM((tm, D), BF16), pltpu.VMEM((tm, D), F32)],
        compiler_params=_cparams(("parallel", "arbitrary")),
        name="ffn_first",
    )(xc, xl, pos, modss, gain, w_in, w_in, w_out)


def _ffn_last_call(x, modss, gain, w_in, w_out, gain2, *, tm=1024, tf=512):
    n_f = D_FF // tf
    ctx_tiles = N_CTX // tm
    kern = functools.partial(_ffn_last_kernel, n_f=n_f, ctx_tiles=ctx_tiles)
    return pl.pallas_call(
        kern,
        out_shape=(jax.ShapeDtypeStruct((N_CTX, D), F32), jax.ShapeDtypeStruct((N_TOK - N_CTX, D), F32)),
        grid=(N_TOK // tm, n_f),
        in_specs=[pl.BlockSpec((tm, D), lambda i, f: (i, 0))] + _ffn_specs(tm, tf) + [
            pl.BlockSpec((1, D), lambda i, f: (0, 0))],
        out_specs=(pl.BlockSpec((tm, D), lambda i, f: (jnp.minimum(i, ctx_tiles - 1), 0)),
                   pl.BlockSpec((tm, D), lambda i, f: (jnp.maximum(i - ctx_tiles, 0), 0))),
        scratch_shapes=[pltpu.VMEM((tm, D), BF16), pltpu.VMEM((tm, D), F32)],
        compiler_params=_cparams(("parallel", "arbitrary")),
        name="ffn_last",
    )(x, modss, gain, w_in, w_in, w_out, gain2)


def _mixin_rwkv_kernel(x_ref, mod_ref, gain_ref, w_ref, mu_ref, o_ref, h_scr):
    i = pl.program_id(0)
    j = pl.program_id(1)

    @pl.when(j == 0)
    def _():
        y = _rms(x_ref[...], gain_ref[...])
        h_scr[...] = (y * (1.0 + mod_ref[4:5, :]) + mod_ref[3:4, :]).astype(BF16)

    u = _dot(h_scr[...], w_ref[...].astype(BF16))
    rows = u.shape[0]
    mu = mu_ref[...]
    half_mu = 0.5 * mu
    o_ref[...] = u * (1.0 - mu) + (pltpu.roll(u, 1, 0) + pltpu.roll(u, rows - 1, 0)) * half_mu

    def drop_outside_neighbours(seq_len):
        for first in range(0, rows, seq_len):
            last = first + seq_len - 1
            before, after = (first - 1) % rows, (last + 1) % rows
            o_ref[first:first + 1, :] = o_ref[first:first + 1, :] - u[before:before + 1, :] * half_mu
            o_ref[last:last + 1, :] = o_ref[last:last + 1, :] - u[after:after + 1, :] * half_mu

    @pl.when(i < N_CTX_SS)
    def _():
        drop_outside_neighbours(CTX_T)

    @pl.when(i >= N_CTX_SS)
    def _():
        drop_outside_neighbours(LAT_T)


def _mixin_rwkv_call(x, modss, gain, w, mu, *, tn=512):
    tm = SS_LEN
    return pl.pallas_call(
        _mixin_rwkv_kernel,
        out_shape=jax.ShapeDtypeStruct((N_TOK, N_RWKV_PAD), F32),
        grid=(N_TOK // tm, N_RWKV_PAD // tn),
        in_specs=[
            pl.BlockSpec((tm, D), lambda i, j: (i, 0)),
            pl.BlockSpec((None, N_MOD, D), lambda i, j: (i, 0, 0)),
            pl.BlockSpec((1, D), lambda i, j: (0, 0)),
            pl.BlockSpec((D, tn), lambda i, j: (0, j)),
            pl.BlockSpec((1, tn), lambda i, j: (0, j)),
        ],
        out_specs=pl.BlockSpec((tm, tn), lambda i, j: (i, j)),
        scratch_shapes=[pltpu.VMEM((tm, D), BF16)],
        compiler_params=_cparams(("parallel", "arbitrary")),
        name="mixin_rwkv",
    )(x, modss, gain, w, mu)


def _mixin_gate_kernel(x_ref, mod_ref, gain_ref, wa_ref, wb_ref, wga_ref, wgb_ref, z_ref, ga_ref, gb_ref,
                       h_scr):
    j = pl.program_id(1)

    @pl.when(j == 0)
    def _():
        y = _rms(x_ref[...], gain_ref[...])
        h_scr[...] = (y * (1.0 + mod_ref[4:5, :]) + mod_ref[3:4, :]).astype(BF16)

    h = h_scr[...]
    z_ref[...] = _dot(h, wa_ref[...]) * _sigmoid(_dot(h, wb_ref[...]))
    ga_ref[...] = _sigmoid(_dot(h, wga_ref[...]))
    gb_ref[...] = _sigmoid(_dot(h, wgb_ref[...]))


def _mixin_gate_call(x, modss, gain, w_glu, w_gate, *, tm=1024, tn=512):
    n_j = D // tn
    ss_per = SS_LEN // tm
    sds = jax.ShapeDtypeStruct((N_TOK, D), F32)
    lo = lambda i, j: (0, j)
    hi = lambda i, j: (0, n_j + j)
    out = pl.BlockSpec((tm, tn), lambda i, j: (i, j))
    return pl.pallas_call(
        _mixin_gate_kernel,
        out_shape=(sds, sds, sds),
        grid=(N_TOK // tm, n_j),
        in_specs=[
            pl.BlockSpec((tm, D), lambda i, j: (i, 0)),
            pl.BlockSpec((None, N_MOD, D), lambda i, j: (i // ss_per, 0, 0)),
            pl.BlockSpec((1, D), lambda i, j: (0, 0)),
            pl.BlockSpec((D, tn), lo),
            pl.BlockSpec((D, tn), hi),
            pl.BlockSpec((D, tn), lo),
            pl.BlockSpec((D, tn), hi),
        ],
        out_specs=(out, out, out),
        scratch_shapes=[pltpu.VMEM((tm, D), BF16)],
        compiler_params=_cparams(("parallel", "arbitrary")),
        name="mixin_gate",
    )(x, modss, gain, w_glu, w_glu, w_gate, w_gate)


def _scan_prep(x, d, reverse, prm):
    (w0_ref, wup_ref, a0_ref, aup_ref, kk_ref, ka_ref, rk_ref, bd_ref) = prm
    c = CHUNK
    r = x[:, 0:D]
    k = x[:, D:2 * D]
    v = x[:, 2 * D:3 * D]
    wd = x[:, 3 * D + d * LORA_W:3 * D + (d + 1) * LORA_W]
    ad_off = 3 * D + 2 * LORA_W
    ad = x[:, ad_off + d * LORA_A:ad_off + (d + 1) * LORA_A]
    bd = bd_ref[...]

    logw = -DECAY_SCALE * _sigmoid(w0_ref[d] + _dot(jnp.tanh(wd).astype(BF16), wup_ref[d]))
    a = _sigmoid(a0_ref[d] + _dot(ad.astype(BF16), aup_ref[d]))
    kk = k * kk_ref[...]
    kk = kk * lax.rsqrt(_head_stat(kk * kk, bd) + 1e-12)
    kd = k * (1.0 + (a - 1.0) * ka_ref[...])
    b = kk * a
    bonus = _head_stat(r * kd * rk_ref[...], bd) * v

    ti = lax.broadcasted_iota(jnp.int32, (c, c), 0)
    si = lax.broadcasted_iota(jnp.int32, (c, c), 1)
    tri = jnp.where((ti <= si) if reverse else (ti >= si), 1.0, 0.0).astype(BF16)
    lw_hi, lw_lo = _split_bf16(logw)
    cum = _dot(tri, lw_hi) + _dot(tri, lw_lo)
    tot = cum[0:1, :] if reverse else cum[c - 1:c, :]
    p_inv = jnp.exp(-cum)
    p_tot = jnp.exp(tot)
    r_b = b * p_inv
    r_k = kd * p_inv
    return dict(
        l_kk=(kk * jnp.exp(cum - logw)).astype(BF16), l_r=(r * jnp.exp(cum)).astype(BF16),
        r_b=r_b, r_k=r_k, s_b=(r_b * p_tot).astype(BF16), s_k=(r_k * p_tot).astype(BF16),
        v=v, p_tot=p_tot, bonus=bonus)


def _group_norm(y, bd, lnw, lnb):
    yc = y - _head_stat(y, bd) * (1.0 / HEAD_DIM)
    var = _head_stat(yc * yc, bd) * (1.0 / HEAD_DIM)
    return yc * lax.rsqrt(var + GN_EPS) * lnw + lnb


def _scan_kernel(xf_ref, xb_ref, s0f_ref, s0b_ref, w0_ref, wup_ref, a0_ref, aup_ref, kk_ref, ka_ref,
                 rk_ref, bd_ref, lnw_ref, lnb_ref, y_ref, sf_ref, sb_ref, s_scr, bonus_scr):
    s = pl.program_id(0)
    is_ctx = s < CTX_STEPS
    n_chunks = jnp.where(is_ctx, CTX_CHUNKS, LAT_CHUNKS)
    t_step = jnp.where(is_ctx, s, s - CTX_STEPS)
    cidx = t_step & (n_chunks - 1)
    seq_in_ss = jnp.where(is_ctx, (t_step >> (CTX_CHUNKS.bit_length() - 1)) & (SS_LEN // CTX_T - 1), 0)
    row_f = pl.multiple_of((seq_in_ss * n_chunks + cidx) * CHUNK, CHUNK)
    row_b = pl.multiple_of((seq_in_ss * n_chunks + n_chunks - 1 - cidx) * CHUNK, CHUNK)
    second_visit = cidx >= (n_chunks >> 1)
    prm = (w0_ref, wup_ref, a0_ref, aup_ref, kk_ref, ka_ref, rk_ref, bd_ref)
    c, g, gw = CHUNK, HEAD_GROUP, HEAD_GROUP * HEAD_DIM
    n_groups = N_HEADS // g
    s0_refs = (s0f_ref, s0b_ref)

    @pl.when(jnp.logical_and(cidx == 0, is_ctx))
    def _():
        s_scr[...] = jnp.zeros_like(s_scr)

    @pl.when(jnp.logical_and(cidx == 0, jnp.logical_not(is_ctx)))
    def _():
        for d in range(2):
            for gi in range(n_groups):
                for l in range(g):
                    pieces = [jnp.zeros((HEAD_DIM, HEAD_DIM), F32)] * g
                    pieces[l] = s0_refs[d][gi * g + l]
                    s_scr[d, gi, l * HEAD_DIM:(l + 1) * HEAD_DIM, :] = jnp.concatenate(pieces, axis=1)

    pre = (_scan_prep(xf_ref[...], 0, False, prm), _scan_prep(xb_ref[...], 1, True, prm))

    row = lax.broadcasted_iota(jnp.int32, (c, gw), 0)
    lane = lax.broadcasted_iota(jnp.int32, (c, gw), 1)
    col = lane & (HEAD_DIM - 1)
    head_of_lane = lane >> 6
    eye = jnp.where(row == col, 1.0, 0.0)
    strict = (row > col, row < col)
    incl = (row >= col, row <= col)
    srow = lax.broadcasted_iota(jnp.int32, (gw, gw), 0) >> 6
    slane = lax.broadcasted_iota(jnp.int32, (gw, gw), 1) >> 6
    state_mask = srow == slane
    nt = (((1,), (1,)), ((), ()))
    tn = (((0,), (0,)), ((), ()))

    def blockdiag(x):
        return jnp.concatenate([jnp.where(head_of_lane == l, x, 0.0) for l in range(g)], axis=0).astype(BF16)

    def same_block(n):
        shift = n.bit_length() - 1
        return (row >> shift) == (col >> shift)

    items = [(d, gi) for d in range(2) for gi in range(n_groups)]
    cols = lambda gi: slice(gi * gw, (gi + 1) * gw)

    s_old = [s_scr[d, gi] for d, gi in items]
    lhs = [jnp.concatenate([pre[d]["l_kk"][:, cols(gi)], pre[d]["l_r"][:, cols(gi)]], axis=0) for d, gi in items]
    rt = [jnp.concatenate([blockdiag(pre[d]["r_b"][:, cols(gi)]), blockdiag(pre[d]["r_k"][:, cols(gi)])], axis=0)
          for d, gi in items]
    amat = [lax.dot_general(lhs[i], rt[i], nt, preferred_element_type=F32) for i in range(len(items))]
    ls = [lax.dot_general(lhs[i], s_old[i].astype(BF16), nt, preferred_element_type=F32) for i in range(len(items))]
    a_ab = [jnp.where(strict[d], amat[i][:c, :gw], 0.0) for i, (d, gi) in enumerate(items)]
    a_ak = [jnp.where(strict[d], amat[i][:c, gw:], 0.0).astype(BF16) for i, (d, gi) in enumerate(items)]
    a_r = [jnp.concatenate([jnp.where(incl[d], amat[i][c:, :gw], 0.0), jnp.where(incl[d], amat[i][c:, gw:], 0.0)],
                           axis=1).astype(BF16) for i, (d, gi) in enumerate(items)]
    bd_v = [blockdiag(pre[d]["v"][:, cols(gi)]) for d, gi in items]
    rhs = [-(ls[i][:c] + _dot(a_ak[i], bd_v[i])) for i in range(len(items))]

    tinv = [eye - jnp.where(same_block(2), a, 0.0) for a in a_ab]
    n = 2
    while n < c:
        lower = jnp.logical_and(same_block(2 * n), jnp.logical_not(same_block(n)))
        off = [jnp.where(lower, a, 0.0).astype(BF16) for a in a_ab]
        w = [_dot(off[i], blockdiag(tinv[i])) for i in range(len(items))]
        tinv = [tinv[i] - _dot(tinv[i].astype(BF16), blockdiag(w[i])) for i in range(len(items))]
        n *= 2

    u = [_dot(tinv[i].astype(BF16), blockdiag(rhs[i])) for i in range(len(items))]
    y = [ls[i][c:] + _dot(a_r[i], jnp.concatenate([blockdiag(u[i]), bd_v[i]], axis=0)) for i in range(len(items))]
    for i, (d, gi) in enumerate(items):
        uv = jnp.concatenate([u[i].astype(BF16), pre[d]["v"][:, cols(gi)].astype(BF16)], axis=0)
        sbk = jnp.concatenate([pre[d]["s_b"][:, cols(gi)], pre[d]["s_k"][:, cols(gi)]], axis=0)
        upd = lax.dot_general(uv, sbk, tn, preferred_element_type=F32)
        s_scr[d, gi] = s_old[i] * pre[d]["p_tot"][:, cols(gi)] + jnp.where(state_mask, upd, 0.0)

    visits = ((row_f, jnp.concatenate(y[:n_groups], axis=1), pre[0]["bonus"]),
              (row_b, jnp.concatenate(y[n_groups:], axis=1), pre[1]["bonus"]))

    @pl.when(jnp.logical_not(second_visit))
    def _():
        for rows, y_d, bonus_d in visits:
            y_ref[pl.ds(rows, c), :] = y_d
            bonus_scr[pl.ds(rows, c), :] = bonus_d

    @pl.when(second_visit)
    def _():
        for rows, y_d, bonus_d in visits:
            y_ref[pl.ds(rows, c), :] = y_ref[pl.ds(rows, c), :] + y_d
            bonus_scr[pl.ds(rows, c), :] = bonus_scr[pl.ds(rows, c), :] + bonus_d

    @pl.when(s % (SS_LEN // CHUNK) == SS_LEN // CHUNK - 1)
    def _():
        def norm_block(i, carry):
            rows = pl.ds(pl.multiple_of(i * NORM_ROWS, NORM_ROWS), NORM_ROWS)
            y_ref[rows, :] = (_group_norm(y_ref[rows, :], bd_ref[...], lnw_ref[...], lnb_ref[...])
                              + bonus_scr[rows, :])
            return carry

        lax.fori_loop(0, SS_LEN // NORM_ROWS, norm_block, 0)

    @pl.when(jnp.logical_and(cidx == n_chunks - 1, is_ctx))
    def _():
        for d, out_ref in enumerate((sf_ref, sb_ref)):
            for h in range(N_HEADS):
                gi, l = divmod(h, g)
                hs = slice(l * HEAD_DIM, (l + 1) * HEAD_DIM)
                out_ref[h] = s_scr[d, gi, hs, hs]


def _scan_row_block(s, reverse):
    ctx = s < CTX_STEPS
    n_chunks = jnp.where(ctx, CTX_CHUNKS, LAT_CHUNKS)
    t = jnp.where(ctx, s, s - CTX_STEPS)
    cidx = t & (n_chunks - 1)
    if reverse:
        cidx = n_chunks - 1 - cidx
    base = jnp.where(ctx, 0, CTX_STEPS)
    return base + (t - (t & (n_chunks - 1))) + cidx


def _scan_call(u, s0f, s0b, w0, wup, a0, aup, k_k, k_a, r_k, bd, lnw, lnb):
    assert CTX_STEPS // N_CTX_SS == LAT_CHUNKS, "context and latent super sequences take the same number of steps"
    for n_chunks in (CTX_CHUNKS, LAT_CHUNKS):
        assert n_chunks >= 2 and n_chunks & (n_chunks - 1) == 0, "chunk index arithmetic uses power-of-two counts"
    steps_per_ss = LAT_CHUNKS
    fwd = lambda s: (_scan_row_block(s, False), 0)
    bwd = lambda s: (_scan_row_block(s, True), 0)
    lat_seq = lambda s: (jnp.maximum(s - CTX_STEPS, 0) // LAT_CHUNKS, 0, 0, 0)
    ctx_seq = lambda s: (jnp.minimum(s // CTX_CHUNKS, CTX_B - 1), 0, 0, 0)
    const2 = lambda s: (0, 0)
    const3 = lambda s: (0, 0, 0)
    st_block = (None, N_HEADS, HEAD_DIM, HEAD_DIM)
    y_sds = jax.ShapeDtypeStruct((N_TOK, D), F32)
    s_sds = jax.ShapeDtypeStruct((CTX_B, N_HEADS, HEAD_DIM, HEAD_DIM), F32)
    return pl.pallas_call(
        _scan_kernel,
        out_shape=(y_sds, s_sds, s_sds),
        grid=(CTX_STEPS + LAT_STEPS,),
        in_specs=[
            pl.BlockSpec((CHUNK, N_RWKV_COLS), fwd),
            pl.BlockSpec((CHUNK, N_RWKV_COLS), bwd),
            pl.BlockSpec(st_block, lat_seq),
            pl.BlockSpec(st_block, lat_seq),
            pl.BlockSpec((2, 1, D), const3),
            pl.BlockSpec((2, LORA_W, D), const3),
            pl.BlockSpec((2, 1, D), const3),
            pl.BlockSpec((2, LORA_A, D), const3),
            pl.BlockSpec((1, D), const2),
            pl.BlockSpec((1, D), const2),
            pl.BlockSpec((1, D), const2),
            pl.BlockSpec((STAT_W, STAT_W), const2),
            pl.BlockSpec((1, D), const2),
            pl.BlockSpec((1, D), const2),
        ],
        out_specs=(
            pl.BlockSpec((SS_LEN, D), lambda s: (s // steps_per_ss, 0)),
            pl.BlockSpec(st_block, ctx_seq),
            pl.BlockSpec(st_block, ctx_seq),
        ),
        scratch_shapes=[
            pltpu.VMEM((2, N_HEADS // HEAD_GROUP, HEAD_GROUP * HEAD_DIM, HEAD_GROUP * HEAD_DIM), F32),
            pltpu.VMEM((SS_LEN, D), F32),
        ],
        compiler_params=_cparams(("arbitrary",)),
        name="wkv7_scan",
    )(u, u, s0f, s0b, w0, wup, a0, aup, k_k, k_a, r_k, bd, lnw, lnb)


def _convmix_kernel(z_ref, zp_ref, zn_ref, w_ref, b_ref, lnw_ref, lnb_ref, x_ref, mod_ref, y_ref, gd_ref,
                    ga_ref, gb_ref, gup_ref, wo_ref, o_ref, pad_scr, cv_scr, *, tm, halo):
    i = pl.program_id(0)
    tiles_per_lat = LAT_T // tm
    j = (i - N_CTX // tm) % tiles_per_lat
    is_ctx = i < N_CTX // tm
    keep_prev = jnp.where(jnp.logical_or(is_ctx, j == 0), 0.0, 1.0)
    keep_next = jnp.where(jnp.logical_or(is_ctx, j == tiles_per_lat - 1), 0.0, 1.0)
    pad_scr[0, 0:halo, :] = zp_ref[...] * keep_prev
    pad_scr[0, halo:halo + tm, :] = z_ref[...]
    pad_scr[0, halo + tm:halo + tm + halo, :] = zn_ref[...] * keep_next
    span = tm + 2 * halo - SUBLANES
    for m in range(1, SUBLANES):
        pad_scr[m, 0:span, :] = pad_scr[0, m:m + span, :]

    rb = 128
    lanes = 128

    def lane_block(cb, carry):
        cs = pl.ds(pl.multiple_of(cb * lanes, lanes), lanes)
        for r0 in range(0, tm, rb):
            acc = jnp.broadcast_to(b_ref[:, cs], (rb, lanes))
            for tap in range(CONV_WIDTH):
                q, m = divmod(tap + halo - CONV_HALF, SUBLANES)
                start = r0 + q * SUBLANES
                acc = acc + w_ref[tap:tap + 1, cs] * pad_scr[m, start:start + rb, cs]
            cv_scr[r0:r0 + rb, cs] = acc
        return carry

    lax.fori_loop(0, D // lanes, lane_block, 0)
    y = cv_scr[...]
    mean = jnp.mean(y, axis=-1, keepdims=True)
    yc = y - mean
    var = jnp.mean(yc * yc, axis=-1, keepdims=True)
    y = yc * lax.rsqrt(var + LN_EPS) * lnw_ref[...] + lnb_ref[...]
    y_conv = y * _sigmoid(y)

    g = _dot(_sigmoid(gd_ref[...]).astype(BF16), gup_ref[...])
    mixed = ga_ref[...] * (y_ref[...] * g) + gb_ref[...] * y_conv
    o_ref[...] = x_ref[...] + mod_ref[5:6, :] * _dot(mixed.astype(BF16), wo_ref[...])


def _convmix_call(z, w, b, lnw, lnb, x, modss, y_att, u, ga, gb, g_up, w_out, *, tm=CTX_T, halo=16):
    assert tm == CTX_T and LAT_T % tm == 0
    hb = tm // halo
    n_h = N_TOK // halo
    ss_per = SS_LEN // tm
    tok = lambda i: (i, 0)
    const2 = lambda i: (0, 0)
    gd_blk = (3 * D + 2 * LORA_W + 2 * LORA_A) // LORA_G
    kern = functools.partial(_convmix_kernel, tm=tm, halo=halo)
    return pl.pallas_call(
        kern,
        out_shape=jax.ShapeDtypeStruct((N_TOK, D), F32),
        grid=(N_TOK // tm,),
        in_specs=[
            pl.BlockSpec((tm, D), tok),
            pl.BlockSpec((halo, D), lambda i: (jnp.maximum(i * hb - 1, 0), 0)),
            pl.BlockSpec((halo, D), lambda i: (jnp.minimum((i + 1) * hb, n_h - 1), 0)),
            pl.BlockSpec((CONV_WIDTH, D), const2),
            pl.BlockSpec((1, D), const2),
            pl.BlockSpec((1, D), const2),
            pl.BlockSpec((1, D), const2),
            pl.BlockSpec((tm, D), tok),
            pl.BlockSpec((None, N_MOD, D), lambda i: (i // ss_per, 0, 0)),
            pl.BlockSpec((tm, D), tok),
            pl.BlockSpec((tm, LORA_G), lambda i: (i, gd_blk)),
            pl.BlockSpec((tm, D), tok),
            pl.BlockSpec((tm, D), tok),
            pl.BlockSpec((LORA_G, D), const2),
            pl.BlockSpec((D, D), const2),
        ],
        out_specs=pl.BlockSpec((tm, D), tok),
        scratch_shapes=[pltpu.VMEM((SUBLANES, tm + 2 * halo, D), F32), pltpu.VMEM((tm, D), F32)],
        compiler_params=_cparams(("parallel",)),
        name="conv_mixout",
    )(z, z, z, w, b, lnw, lnb, x, modss, y_att, u, ga, gb, g_up, w_out)


def _grid_pos_embed(rows):
    n_freq = D // 4
    freqs = jnp.exp(-math.log(POS_MAX_PERIOD) * jnp.arange(n_freq, dtype=F32) / n_freq)
    ang_r = jnp.arange(rows, dtype=F32)[:, None] * freqs
    ang_c = jnp.arange(GRID_W, dtype=F32)[:, None] * freqs
    emb_r = jnp.concatenate([jnp.sin(ang_r), jnp.cos(ang_r)], axis=-1)
    emb_c = jnp.concatenate([jnp.sin(ang_c), jnp.cos(ang_c)], axis=-1)
    return jnp.concatenate([jnp.repeat(emb_r, GRID_W, axis=0), jnp.tile(emb_c, (rows, 1))], axis=-1)


def kernel(x_prompt, x_sample, state_rwkv_fwd, state_rwkv_bwd, c, c_ctx, w_mod, b_mod, norm_ffn1, ffn1_w_in, ffn1_w_out, norm_mix, mix_w_in, rwkv_mu, rwkv_w0, rwkv_w_up, rwkv_a0, rwkv_a_up, rwkv_g_up, rwkv_k_k, rwkv_k_a, rwkv_r_k, rwkv_lnx_w, rwkv_lnx_b, conv_dw, conv_db, conv_ln_w, conv_ln_b, mix_w_out, norm_ffn2, ffn2_w_in, ffn2_w_out, norm_final):
    assert x_prompt.shape == (CTX_B, CTX_T, D) and x_sample.shape == (LAT_B, LAT_T, D)
    assert w_mod.shape[0] == 1, "single-layer trunk"
    row = lambda t: t.reshape(1, -1)

    cond = jnp.concatenate([c, c_ctx[None], jnp.zeros((8 - LAT_B - 1, D), F32)], axis=0)
    mod = _mod_call(cond, w_mod[0], row(b_mod[0])).reshape(8, N_MOD, D)
    ss_cond = jnp.array([LAT_B] * N_CTX_SS + list(range(LAT_B)), jnp.int32)
    modss = mod[ss_cond]

    pos = _grid_pos_embed(LAT_T // GRID_W).astype(x_sample.dtype)

    w_mix = mix_w_in[0]
    glu_off = N_RWKV_COLS
    gate_off = N_RWKV_COLS + 2 * D
    col_pad = ((0, 0), (0, N_RWKV_PAD - N_RWKV_COLS))
    w_glu = w_mix[:, glu_off:gate_off].astype(BF16)
    w_gate = w_mix[:, gate_off:].astype(BF16)

    x1 = _ffn_first_call(x_prompt.reshape(N_CTX, D), x_sample.reshape(LAT_B * LAT_T, D), pos, modss,
                         row(norm_ffn1[0]), ffn1_w_in[0], ffn1_w_out[0])

    u = _mixin_rwkv_call(x1, modss, row(norm_mix[0]), w_mix, jnp.pad(row(rwkv_mu[0]), col_pad))
    z, ga, gb = _mixin_gate_call(x1, modss, row(norm_mix[0]), w_glu, w_gate)

    ii = lax.broadcasted_iota(jnp.int32, (STAT_W, STAT_W), 0) // HEAD_DIM
    jj = lax.broadcasted_iota(jnp.int32, (STAT_W, STAT_W), 1) // HEAD_DIM
    bd = jnp.where(ii == jj, 1.0, 0.0).astype(BF16)

    y_att, s_f, s_b = _scan_call(
        u, state_rwkv_fwd[:, 0], state_rwkv_bwd[:, 0],
        rwkv_w0[0].reshape(2, 1, D), rwkv_w_up[0].astype(BF16), rwkv_a0[0].reshape(2, 1, D),
        rwkv_a_up[0].astype(BF16), row(rwkv_k_k[0]), row(rwkv_k_a[0]), row(rwkv_r_k[0]), bd,
        row(rwkv_lnx_w[0]), row(rwkv_lnx_b[0]))

    x2 = _convmix_call(z, conv_dw[0], row(conv_db[0]), row(conv_ln_w[0]), row(conv_ln_b[0]),
                       x1, modss, y_att, u, ga, gb, rwkv_g_up[0].astype(BF16), mix_w_out[0].astype(BF16))

    y_ctx, y_lat = _ffn_last_call(x2, modss, row(norm_ffn2[0]), ffn2_w_in[0], ffn2_w_out[0], row(norm_final))

    st = (CTX_B, 1, N_HEADS, HEAD_DIM, HEAD_DIM)
    return (y_ctx.reshape(CTX_B, CTX_T, D), y_lat.reshape(LAT_B, LAT_T, D),
            s_f.reshape(st).astype(x_prompt.dtype), s_b.reshape(st).astype(x_prompt.dtype))
```

```python
import functools
import math

import jax
import jax.numpy as jnp
from jax import lax
from jax.experimental import pallas as pl
from jax.experimental.pallas import tpu as pltpu

F32 = jnp.float32
BF16 = jnp.bfloat16

D = 1024
N_HEADS = 16
HEAD_DIM = 64
D_FF = 4 * D
N_MOD = 9
CONV_WIDTH = 31
CONV_HALF = CONV_WIDTH // 2
LORA_W = 64
LORA_A = 64
LORA_G = 128
N_RWKV_COLS = 3 * D + 2 * LORA_W + 2 * LORA_A + LORA_G
N_RWKV_PAD = 3584
GRID_W = 64
EPS = 1e-6
LN_EPS = 1e-5
GN_EPS = 64e-5
DECAY_SCALE = 0.606531
POS_MAX_PERIOD = 10000.0

CTX_B, CTX_T = 32, 256
LAT_B, LAT_T = 4, 2048
N_CTX = CTX_B * CTX_T
N_TOK = N_CTX + LAT_B * LAT_T
SS_LEN = 2048
N_CTX_SS = N_CTX // SS_LEN

CHUNK = 64
HEAD_GROUP = 2
CTX_CHUNKS = CTX_T // CHUNK
LAT_CHUNKS = LAT_T // CHUNK
CTX_STEPS = CTX_B * CTX_CHUNKS
LAT_STEPS = LAT_B * LAT_CHUNKS
STAT_W = 256
NORM_ROWS = 256
SUBLANES = 8

VMEM_LIMIT = 58 * 1024 * 1024


def _cparams(sem):
    return pltpu.CompilerParams(dimension_semantics=sem, vmem_limit_bytes=VMEM_LIMIT)


def _sigmoid(x):
    return 1.0 / (1.0 + jnp.exp(-x))


def _dot(a, b):
    return jnp.dot(a, b, preferred_element_type=F32)


def _split_bf16(x):
    hi = x.astype(BF16)
    lo = (x - hi.astype(F32)).astype(BF16)
    return hi, lo


def _head_stat(x, bd):
    groups = D // STAT_W
    stacked = jnp.concatenate([x[:, g * STAT_W:(g + 1) * STAT_W] for g in range(groups)], axis=0)
    s = _dot(stacked.astype(BF16), bd)
    rows = x.shape[0]
    return jnp.concatenate([s[g * rows:(g + 1) * rows] for g in range(groups)], axis=1)


def _mod_kernel(cond_ref, w_ref, b_ref, o_ref):
    c = cond_ref[...]
    s = (c * _sigmoid(c)).astype(BF16)
    o_ref[...] = _dot(s, w_ref[...].astype(BF16)) + b_ref[...]


def _mod_call(cond, w_mod, b_mod):
    tn = 1152
    n = N_MOD * D
    return pl.pallas_call(
        _mod_kernel,
        out_shape=jax.ShapeDtypeStruct((8, n), F32),
        grid=(n // tn,),
        in_specs=[
            pl.BlockSpec((8, D), lambda j: (0, 0)),
            pl.BlockSpec((D, tn), lambda j: (0, j)),
            pl.BlockSpec((1, tn), lambda j: (0, j)),
        ],
        out_specs=pl.BlockSpec((8, tn), lambda j: (0, j)),
        compiler_params=_cparams(("arbitrary",)),
        name="mod",
    )(cond, w_mod, b_mod)


def _rms(x, gain):
    return x * lax.rsqrt(jnp.mean(x * x, axis=-1, keepdims=True) + EPS) * gain


def _ffn_start(x, mod_ref, gain_ref, h_scr, acc_scr, mod_row):
    y = _rms(x, gain_ref[...])
    h = y * (1.0 + mod_ref[mod_row + 1:mod_row + 2, :]) + mod_ref[mod_row:mod_row + 1, :]
    h_scr[...] = h.astype(BF16)
    acc_scr[...] = jnp.zeros_like(acc_scr)


def _ffn_step(wg_ref, wu_ref, wo_ref, h_scr, acc_scr):
    h = h_scr[...]
    tf = wg_ref.shape[1]
    part = tf // 2
    update = None
    for p in range(2):
        cs = slice(p * part, (p + 1) * part)
        g = _dot(h, wg_ref[:, cs].astype(BF16))
        u = _dot(h, wu_ref[:, cs].astype(BF16))
        a = (g * _sigmoid(g) * u).astype(BF16)
        d = _dot(a, wo_ref[cs, :].astype(BF16))
        update = d if update is None else update + d
    acc_scr[...] += update


def _ffn_first_kernel(xc_ref, xl_ref, pos_ref, mod_ref, gain_ref, wg_ref, wu_ref, wo_ref, o_ref,
                      h_scr, acc_scr, *, n_f, ctx_tiles):
    i = pl.program_id(0)
    f = pl.program_id(1)
    is_ctx = i < ctx_tiles
    sources = ((is_ctx, lambda: xc_ref[...]), (jnp.logical_not(is_ctx), lambda: xl_ref[...] + pos_ref[...]))

    for cond, x_in in sources:
        @pl.when(jnp.logical_and(f == 0, cond))
        def _():
            _ffn_start(x_in(), mod_ref, gain_ref, h_scr, acc_scr, 0)

    _ffn_step(wg_ref, wu_ref, wo_ref, h_scr, acc_scr)

    for cond, x_in in sources:
        @pl.when(jnp.logical_and(f == n_f - 1, cond))
        def _():
            o_ref[...] = x_in() + mod_ref[2:3, :] * (0.5 * acc_scr[...])


def _ffn_last_kernel(x_ref, mod_ref, gain_ref, wg_ref, wu_ref, wo_ref, gain2_ref, oc_ref, ol_ref,
                     h_scr, acc_scr, *, n_f, ctx_tiles):
    i = pl.program_id(0)
    f = pl.program_id(1)

    @pl.when(f == 0)
    def _():
        _ffn_start(x_ref[...], mod_ref, gain_ref, h_scr, acc_scr, 6)

    _ffn_step(wg_ref, wu_ref, wo_ref, h_scr, acc_scr)

    def result():
        return _rms(x_ref[...] + mod_ref[8:9, :] * (0.5 * acc_scr[...]), gain2_ref[...])

    @pl.when(jnp.logical_and(f == n_f - 1, i < ctx_tiles))
    def _():
        oc_ref[...] = result()

    @pl.when(jnp.logical_and(f == n_f - 1, i >= ctx_tiles))
    def _():
        ol_ref[...] = result()


def _ffn_specs(tm, tf):
    n_f = D_FF // tf
    ss_per = SS_LEN // tm
    return [
        pl.BlockSpec((None, N_MOD, D), lambda i, f: (i // ss_per, 0, 0)),
        pl.BlockSpec((1, D), lambda i, f: (0, 0)),
        pl.BlockSpec((D, tf), lambda i, f: (0, f)),
        pl.BlockSpec((D, tf), lambda i, f: (0, n_f + f)),
        pl.BlockSpec((tf, D), lambda i, f: (f, 0)),
    ]


def _ffn_first_call(xc, xl, pos, modss, gain, w_in, w_out, *, tm=1024, tf=512):
    n_f = D_FF // tf
    ctx_tiles = N_CTX // tm
    pos_tiles = LAT_T // tm
    kern = functools.partial(_ffn_first_kernel, n_f=n_f, ctx_tiles=ctx_tiles)
    return pl.pallas_call(
        kern,
        out_shape=jax.ShapeDtypeStruct((N_TOK, D), F32),
        grid=(N_TOK // tm, n_f),
        in_specs=[
            pl.BlockSpec((tm, D), lambda i, f: (jnp.minimum(i, ctx_tiles - 1), 0)),
            pl.BlockSpec((tm, D), lambda i, f: (jnp.maximum(i - ctx_tiles, 0), 0)),
            pl.BlockSpec((tm, D), lambda i, f: (jnp.maximum(i - ctx_tiles, 0) % pos_tiles, 0)),
        ] + _ffn_specs(tm, tf),
        out_specs=pl.BlockSpec((tm, D), lambda i, f: (i, 0)),
        scratch_shapes=[pltpu.VMEM((tm, D), BF16), pltpu.VMEM((tm, D), F32)],
        compiler_params=_cparams(("parallel", "arbitrary")),
        name="ffn_first",
    )(xc, xl, pos, modss, gain, w_in, w_in, w_out)


def _ffn_last_call(x, modss, gain, w_in, w_out, gain2, *, tm=1024, tf=512):
    n_f = D_FF // tf
    ctx_tiles = N_CTX // tm
    kern = functools.partial(_ffn_last_kernel, n_f=n_f, ctx_tiles=ctx_tiles)
    return pl.pallas_call(
        kern,
        out_shape=(jax.ShapeDtypeStruct((N_CTX, D), F32), jax.ShapeDtypeStruct((N_TOK - N_CTX, D), F32)),
        grid=(N_TOK // tm, n_f),
        in_specs=[pl.BlockSpec((tm, D), lambda i, f: (i, 0))] + _ffn_specs(tm, tf) + [
            pl.BlockSpec((1, D), lambda i, f: (0, 0))],
        out_specs=(pl.BlockSpec((tm, D), lambda i, f: (jnp.minimum(i, ctx_tiles - 1), 0)),
                   pl.BlockSpec((tm, D), lambda i, f: (jnp.maximum(i - ctx_tiles, 0), 0))),
        scratch_shapes=[pltpu.VMEM((tm, D), BF16), pltpu.VMEM((tm, D), F32)],
        compiler_params=_cparams(("parallel", "arbitrary")),
        name="ffn_last",
    )(x, modss, gain, w_in, w_in, w_out, gain2)


def _mixin_rwkv_kernel(x_ref, mod_ref, gain_ref, w_ref, mu_ref, o_ref, h_scr):
    i = pl.program_id(0)
    j = pl.program_id(1)

    @pl.when(j == 0)
    def _():
        y = _rms(x_ref[...], gain_ref[...])
        h_scr[...] = (y * (1.0 + mod_ref[4:5, :]) + mod_ref[3:4, :]).astype(BF16)

    u = _dot(h_scr[...], w_ref[...].astype(BF16))
    rows = u.shape[0]
    mu = mu_ref[...]
    half_mu = 0.5 * mu
    o_ref[...] = u * (1.0 - mu) + (pltpu.roll(u, 1, 0) + pltpu.roll(u, rows - 1, 0)) * half_mu

    def drop_outside_neighbours(seq_len):
        for first in range(0, rows, seq_len):
            last = first + seq_len - 1
            before, after = (first - 1) % rows, (last + 1) % rows
            o_ref[first:first + 1, :] = o_ref[first:first + 1, :] - u[before:before + 1, :] * half_mu
            o_ref[last:last + 1, :] = o_ref[last:last + 1, :] - u[after:after + 1, :] * half_mu

    @pl.when(i < N_CTX_SS)
    def _():
        drop_outside_neighbours(CTX_T)

    @pl.when(i >= N_CTX_SS)
    def _():
        drop_outside_neighbours(LAT_T)


def _mixin_rwkv_call(x, modss, gain, w, mu, *, tn=512):
    tm = SS_LEN
    return pl.pallas_call(
        _mixin_rwkv_kernel,
        out_shape=jax.ShapeDtypeStruct((N_TOK, N_RWKV_PAD), F32),
        grid=(N_TOK // tm, N_RWKV_PAD // tn),
        in_specs=[
            pl.BlockSpec((tm, D), lambda i, j: (i, 0)),
            pl.BlockSpec((None, N_MOD, D), lambda i, j: (i, 0, 0)),
            pl.BlockSpec((1, D), lambda i, j: (0, 0)),
            pl.BlockSpec((D, tn), lambda i, j: (0, j)),
            pl.BlockSpec((1, tn), lambda i, j: (0, j)),
        ],
        out_specs=pl.BlockSpec((tm, tn), lambda i, j: (i, j)),
        scratch_shapes=[pltpu.VMEM((tm, D), BF16)],
        compiler_params=_cparams(("parallel", "arbitrary")),
        name="mixin_rwkv",
    )(x, modss, gain, w, mu)


def _mixin_gate_kernel(x_ref, mod_ref, gain_ref, wa_ref, wb_ref, wga_ref, wgb_ref, z_ref, ga_ref, gb_ref,
                       h_scr):
    j = pl.program_id(1)

    @pl.when(j == 0)
    def _():
        y = _rms(x_ref[...], gain_ref[...])
        h_scr[...] = (y * (1.0 + mod_ref[4:5, :]) + mod_ref[3:4, :]).astype(BF16)

    h = h_scr[...]
    z_ref[...] = _dot(h, wa_ref[...]) * _sigmoid(_dot(h, wb_ref[...]))
    ga_ref[...] = _sigmoid(_dot(h, wga_ref[...]))
    gb_ref[...] = _sigmoid(_dot(h, wgb_ref[...]))


def _mixin_gate_call(x, modss, gain, w_glu, w_gate, *, tm=1024, tn=512):
    n_j = D // tn
    ss_per = SS_LEN // tm
    sds = jax.ShapeDtypeStruct((N_TOK, D), F32)
    lo = lambda i, j: (0, j)
    hi = lambda i, j: (0, n_j + j)
    out = pl.BlockSpec((tm, tn), lambda i, j: (i, j))
    return pl.pallas_call(
        _mixin_gate_kernel,
        out_shape=(sds, sds, sds),
        grid=(N_TOK // tm, n_j),
        in_specs=[
            pl.BlockSpec((tm, D), lambda i, j: (i, 0)),
            pl.BlockSpec((None, N_MOD, D), lambda i, j: (i // ss_per, 0, 0)),
            pl.BlockSpec((1, D), lambda i, j: (0, 0)),
            pl.BlockSpec((D, tn), lo),
            pl.BlockSpec((D, tn), hi),
            pl.BlockSpec((D, tn), lo),
            pl.BlockSpec((D, tn), hi),
        ],
        out_specs=(out, out, out),
        scratch_shapes=[pltpu.VMEM((tm, D), BF16)],
        compiler_params=_cparams(("parallel", "arbitrary")),
        name="mixin_gate",
    )(x, modss, gain, w_glu, w_glu, w_gate, w_gate)


def _scan_prep(x, d, reverse, prm):
    (w0_ref, wup_ref, a0_ref, aup_ref, kk_ref, ka_ref, rk_ref, bd_ref) = prm
    c = CHUNK
    r = x[:, 0:D]
    k = x[:, D:2 * D]
    v = x[:, 2 * D:3 * D]
    wd = x[:, 3 * D + d * LORA_W:3 * D + (d + 1) * LORA_W]
    ad_off = 3 * D + 2 * LORA_W
    ad = x[:, ad_off + d * LORA_A:ad_off + (d + 1) * LORA_A]
    bd = bd_ref[...]

    logw = -DECAY_SCALE * _sigmoid(w0_ref[d] + _dot(jnp.tanh(wd).astype(BF16), wup_ref[d]))
    a = _sigmoid(a0_ref[d] + _dot(ad.astype(BF16), aup_ref[d]))
    kk = k * kk_ref[...]
    kk = kk * lax.rsqrt(_head_stat(kk * kk, bd) + 1e-12)
    kd = k * (1.0 + (a - 1.0) * ka_ref[...])
    b = kk * a
    bonus = _head_stat(r * kd * rk_ref[...], bd) * v

    ti = lax.broadcasted_iota(jnp.int32, (c, c), 0)
    si = lax.broadcasted_iota(jnp.int32, (c, c), 1)
    tri = jnp.where((ti <= si) if reverse else (ti >= si), 1.0, 0.0).astype(BF16)
    lw_hi, lw_lo = _split_bf16(logw)
    cum = _dot(tri, lw_hi) + _dot(tri, lw_lo)
    tot = cum[0:1, :] if reverse else cum[c - 1:c, :]
    p_inv = jnp.exp(-cum)
    p_tot = jnp.exp(tot)
    r_b = b * p_inv
    r_k = kd * p_inv
    return dict(
        l_kk=(kk * jnp.exp(cum - logw)).astype(BF16), l_r=(r * jnp.exp(cum)).astype(BF16),
        r_b=r_b, r_k=r_k, s_b=(r_b * p_tot).astype(BF16), s_k=(r_k * p_tot).astype(BF16),
        v=v, p_tot=p_tot, bonus=bonus)


def _group_norm(y, bd, lnw, lnb):
    yc = y - _head_stat(y, bd) * (1.0 / HEAD_DIM)
    var = _head_stat(yc * yc, bd) * (1.0 / HEAD_DIM)
    return yc * lax.rsqrt(var + GN_EPS) * lnw + lnb


def _scan_kernel(xf_ref, xb_ref, s0f_ref, s0b_ref, w0_ref, wup_ref, a0_ref, aup_ref, kk_ref, ka_ref,
                 rk_ref, bd_ref, lnw_ref, lnb_ref, y_ref, sf_ref, sb_ref, s_scr, bonus_scr):
    s = pl.program_id(0)
    is_ctx = s < CTX_STEPS
    n_chunks = jnp.where(is_ctx, CTX_CHUNKS, LAT_CHUNKS)
    t_step = jnp.where(is_ctx, s, s - CTX_STEPS)
    cidx = t_step & (n_chunks - 1)
    seq_in_ss = jnp.where(is_ctx, (t_step >> (CTX_CHUNKS.bit_length() - 1)) & (SS_LEN // CTX_T - 1), 0)
    row_f = pl.multiple_of((seq_in_ss * n_chunks + cidx) * CHUNK, CHUNK)
    row_b = pl.multiple_of((seq_in_ss * n_chunks + n_chunks - 1 - cidx) * CHUNK, CHUNK)
    second_visit = cidx >= (n_chunks >> 1)
    prm = (w0_ref, wup_ref, a0_ref, aup_ref, kk_ref, ka_ref, rk_ref, bd_ref)
    c, g, gw = CHUNK, HEAD_GROUP, HEAD_GROUP * HEAD_DIM
    n_groups = N_HEADS // g
    s0_refs = (s0f_ref, s0b_ref)

    @pl.when(jnp.logical_and(cidx == 0, is_ctx))
    def _():
        s_scr[...] = jnp.zeros_like(s_scr)

    @pl.when(jnp.logical_and(cidx == 0, jnp.logical_not(is_ctx)))
    def _():
        for d in range(2):
            for gi in range(n_groups):
                for l in range(g):
                    pieces = [jnp.zeros((HEAD_DIM, HEAD_DIM), F32)] * g
                    pieces[l] = s0_refs[d][gi * g + l]
                    s_scr[d, gi, l * HEAD_DIM:(l + 1) * HEAD_DIM, :] = jnp.concatenate(pieces, axis=1)

    pre = (_scan_prep(xf_ref[...], 0, False, prm), _scan_prep(xb_ref[...], 1, True, prm))

    row = lax.broadcasted_iota(jnp.int32, (c, gw), 0)
    lane = lax.broadcasted_iota(jnp.int32, (c, gw), 1)
    col = lane & (HEAD_DIM - 1)
    head_of_lane = lane >> 6
    eye = jnp.where(row == col, 1.0, 0.0)
    strict = (row > col, row < col)
    incl = (row >= col, row <= col)
    srow = lax.broadcasted_iota(jnp.int32, (gw, gw), 0) >> 6
    slane = lax.broadcasted_iota(jnp.int32, (gw, gw), 1) >> 6
    state_mask = srow == slane
    nt = (((1,), (1,)), ((), ()))
    tn = (((0,), (0,)), ((), ()))

    def blockdiag(x):
        return jnp.concatenate([jnp.where(head_of_lane == l, x, 0.0) for l in range(g)], axis=0).astype(BF16)

    def same_block(n):
        shift = n.bit_length() - 1
        return (row >> shift) == (col >> shift)

    items = [(d, gi) for d in range(2) for gi in range(n_groups)]
    cols = lambda gi: slice(gi * gw, (gi + 1) * gw)

    s_old = [s_scr[d, gi] for d, gi in items]
    lhs = [jnp.concatenate([pre[d]["l_kk"][:, cols(gi)], pre[d]["l_r"][:, cols(gi)]], axis=0) for d, gi in items]
    rt = [jnp.concatenate([blockdiag(pre[d]["r_b"][:, cols(gi)]), blockdiag(pre[d]["r_k"][:, cols(gi)])], axis=0)
          for d, gi in items]
    amat = [lax.dot_general(lhs[i], rt[i], nt, preferred_element_type=F32) for i in range(len(items))]
    ls = [lax.dot_general(lhs[i], s_old[i].astype(BF16), nt, preferred_element_type=F32) for i in range(len(items))]
    a_ab = [jnp.where(strict[d], amat[i][:c, :gw], 0.0) for i, (d, gi) in enumerate(items)]
    a_ak = [jnp.where(strict[d], amat[i][:c, gw:], 0.0).astype(BF16) for i, (d, gi) in enumerate(items)]
    a_r = [jnp.concatenate([jnp.where(incl[d], amat[i][c:, :gw], 0.0), jnp.where(incl[d], amat[i][c:, gw:], 0.0)],
                           axis=1).astype(BF16) for i, (d, gi) in enumerate(items)]
    bd_v = [blockdiag(pre[d]["v"][:, cols(gi)]) for d, gi in items]
    rhs = [-(ls[i][:c] + _dot(a_ak[i], bd_v[i])) for i in range(len(items))]

    tinv = [eye - jnp.where(same_block(2), a, 0.0) for a in a_ab]
    n = 2
    while n < c:
        lower = jnp.logical_and(same_block(2 * n), jnp.logical_not(same_block(n)))
        off = [jnp.where(lower, a, 0.0).astype(BF16) for a in a_ab]
        w = [_dot(off[i], blockdiag(tinv[i])) for i in range(len(items))]
        tinv = [tinv[i] - _dot(tinv[i].astype(BF16), blockdiag(w[i])) for i in range(len(items))]
        n *= 2

    u = [_dot(tinv[i].astype(BF16), blockdiag(rhs[i])) for i in range(len(items))]
    y = [ls[i][c:] + _dot(a_r[i], jnp.concatenate([blockdiag(u[i]), bd_v[i]], axis=0)) for i in range(len(items))]
    for i, (d, gi) in enumerate(items):
        uv = jnp.concatenate([u[i].astype(BF16), pre[d]["v"][:, cols(gi)].astype(BF16)], axis=0)
        sbk = jnp.concatenate([pre[d]["s_b"][:, cols(gi)], pre[d]["s_k"][:, cols(gi)]], axis=0)
        upd = lax.dot_general(uv, sbk, tn, preferred_element_type=F32)
        s_scr[d, gi] = s_old[i] * pre[d]["p_tot"][:, cols(gi)] + jnp.where(state_mask, upd, 0.0)

    visits = ((row_f, jnp.concatenate(y[:n_groups], axis=1), pre[0]["bonus"]),
              (row_b, jnp.concatenate(y[n_groups:], axis=1), pre[1]["bonus"]))

    @pl.when(jnp.logical_not(second_visit))
    def _():
        for rows, y_d, bonus_d in visits:
            y_ref[pl.ds(rows, c), :] = y_d
            bonus_scr[pl.ds(rows, c), :] = bonus_d

    @pl.when(second_visit)
    def _():
        for rows, y_d, bonus_d in visits:
            y_ref[pl.ds(rows, c), :] = y_ref[pl.ds(rows, c), :] + y_d
            bonus_scr[pl.ds(rows, c), :] = bonus_scr[pl.ds(rows, c), :] + bonus_d

    @pl.when(s % (SS_LEN // CHUNK) == SS_LEN // CHUNK - 1)
    def _():
        def norm_block(i, carry):
            rows = pl.ds(pl.multiple_of(i * NORM_ROWS, NORM_ROWS), NORM_ROWS)
            y_ref[rows, :] = (_group_norm(y_ref[rows, :], bd_ref[...], lnw_ref[...], lnb_ref[...])
                              + bonus_scr[rows, :])
            return carry

        lax.fori_loop(0, SS_LEN // NORM_ROWS, norm_block, 0)

    @pl.when(jnp.logical_and(cidx == n_chunks - 1, is_ctx))
    def _():
        for d, out_ref in enumerate((sf_ref, sb_ref)):
            for h in range(N_HEADS):
                gi, l = divmod(h, g)
                hs = slice(l * HEAD_DIM, (l + 1) * HEAD_DIM)
                out_ref[h] = s_scr[d, gi, hs, hs]


def _scan_row_block(s, reverse):
    ctx = s < CTX_STEPS
    n_chunks = jnp.where(ctx, CTX_CHUNKS, LAT_CHUNKS)
    t = jnp.where(ctx, s, s - CTX_STEPS)
    cidx = t & (n_chunks - 1)
    if reverse:
        cidx = n_chunks - 1 - cidx
    base = jnp.where(ctx, 0, CTX_STEPS)
    return base + (t - (t & (n_chunks - 1))) + cidx


def _scan_call(u, s0f, s0b, w0, wup, a0, aup, k_k, k_a, r_k, bd, lnw, lnb):
    assert CTX_STEPS // N_CTX_SS == LAT_CHUNKS, "context and latent super sequences take the same number of steps"
    for n_chunks in (CTX_CHUNKS, LAT_CHUNKS):
        assert n_chunks >= 2 and n_chunks & (n_chunks - 1) == 0, "chunk index arithmetic uses power-of-two counts"
    steps_per_ss = LAT_CHUNKS
    fwd = lambda s: (_scan_row_block(s, False), 0)
    bwd = lambda s: (_scan_row_block(s, True), 0)
    lat_seq = lambda s: (jnp.maximum(s - CTX_STEPS, 0) // LAT_CHUNKS, 0, 0, 0)
    ctx_seq = lambda s: (jnp.minimum(s // CTX_CHUNKS, CTX_B - 1), 0, 0, 0)
    const2 = lambda s: (0, 0)
    const3 = lambda s: (0, 0, 0)
    st_block = (None, N_HEADS, HEAD_DIM, HEAD_DIM)
    y_sds = jax.ShapeDtypeStruct((N_TOK, D), F32)
    s_sds = jax.ShapeDtypeStruct((CTX_B, N_HEADS, HEAD_DIM, HEAD_DIM), F32)
    return pl.pallas_call(
        _scan_kernel,
        out_shape=(y_sds, s_sds, s_sds),
        grid=(CTX_STEPS + LAT_STEPS,),
        in_specs=[
            pl.BlockSpec((CHUNK, N_RWKV_COLS), fwd),
            pl.BlockSpec((CHUNK, N_RWKV_COLS), bwd),
            pl.BlockSpec(st_block, lat_seq),
            pl.BlockSpec(st_block, lat_seq),
            pl.BlockSpec((2, 1, D), const3),
            pl.BlockSpec((2, LORA_W, D), const3),
            pl.BlockSpec((2, 1, D), const3),
            pl.BlockSpec((2, LORA_A, D), const3),
            pl.BlockSpec((1, D), const2),
            pl.BlockSpec((1, D), const2),
            pl.BlockSpec((1, D), const2),
            pl.BlockSpec((STAT_W, STAT_W), const2),
            pl.BlockSpec((1, D), const2),
            pl.BlockSpec((1, D), const2),
        ],
        out_specs=(
            pl.BlockSpec((SS_LEN, D), lambda s: (s // steps_per_ss, 0)),
            pl.BlockSpec(st_block, ctx_seq),
            pl.BlockSpec(st_block, ctx_seq),
        ),
        scratch_shapes=[
            pltpu.VMEM((2, N_HEADS // HEAD_GROUP, HEAD_GROUP * HEAD_DIM, HEAD_GROUP * HEAD_DIM), F32),
            pltpu.VMEM((SS_LEN, D), F32),
        ],
        compiler_params=_cparams(("arbitrary",)),
        name="wkv7_scan",
    )(u, u, s0f, s0b, w0, wup, a0, aup, k_k, k_a, r_k, bd, lnw, lnb)


def _convmix_kernel(z_ref, zp_ref, zn_ref, w_ref, b_ref, lnw_ref, lnb_ref, x_ref, mod_ref, y_ref, gd_ref,
                    ga_ref, gb_ref, gup_ref, wo_ref, o_ref, pad_scr, cv_scr, *, tm, halo):
    i = pl.program_id(0)
    tiles_per_lat = LAT_T // tm
    j = (i - N_CTX // tm) % tiles_per_lat
    is_ctx = i < N_CTX // tm
    keep_prev = jnp.where(jnp.logical_or(is_ctx, j == 0), 0.0, 1.0)
    keep_next = jnp.where(jnp.logical_or(is_ctx, j == tiles_per_lat - 1), 0.0, 1.0)
    pad_scr[0, 0:halo, :] = zp_ref[...] * keep_prev
    pad_scr[0, halo:halo + tm, :] = z_ref[...]
    pad_scr[0, halo + tm:halo + tm + halo, :] = zn_ref[...] * keep_next
    span = tm + 2 * halo - SUBLANES
    for m in range(1, SUBLANES):
        pad_scr[m, 0:span, :] = pad_scr[0, m:m + span, :]

    rb = 128
    lanes = 128

    def lane_block(cb, carry):
        cs = pl.ds(pl.multiple_of(cb * lanes, lanes), lanes)
        for r0 in range(0, tm, rb):
            acc = jnp.broadcast_to(b_ref[:, cs], (rb, lanes))
            for tap in range(CONV_WIDTH):
                q, m = divmod(tap + halo - CONV_HALF, SUBLANES)
                start = r0 + q * SUBLANES
                acc = acc + w_ref[tap:tap + 1, cs] * pad_scr[m, start:start + rb, cs]
            cv_scr[r0:r0 + rb, cs] = acc
        return carry

    lax.fori_loop(0, D // lanes, lane_block, 0)
    y = cv_scr[...]
    mean = jnp.mean(y, axis=-1, keepdims=True)
    yc = y - mean
    var = jnp.mean(yc * yc, axis=-1, keepdims=True)
    y = yc * lax.rsqrt(var + LN_EPS) * lnw_ref[...] + lnb_ref[...]
    y_conv = y * _sigmoid(y)

    g = _dot(_sigmoid(gd_ref[...]).astype(BF16), gup_ref[...])
    mixed = ga_ref[...] * (y_ref[...] * g) + gb_ref[...] * y_conv
    o_ref[...] = x_ref[...] + mod_ref[5:6, :] * _dot(mixed.astype(BF16), wo_ref[...])


def _convmix_call(z, w, b, lnw, lnb, x, modss, y_att, u, ga, gb, g_up, w_out, *, tm=CTX_T, halo=16):
    assert tm == CTX_T and LAT_T % tm == 0
    hb = tm // halo
    n_h = N_TOK // halo
    ss_per = SS_LEN // tm
    tok = lambda i: (i, 0)
    const2 = lambda i: (0, 0)
    gd_blk = (3 * D + 2 * LORA_W + 2 * LORA_A) // LORA_G
    kern = functools.partial(_convmix_kernel, tm=tm, halo=halo)
    return pl.pallas_call(
        kern,
        out_shape=jax.ShapeDtypeStruct((N_TOK, D), F32),
        grid=(N_TOK // tm,),
        in_specs=[
            pl.BlockSpec((tm, D), tok),
            pl.BlockSpec((halo, D), lambda i: (jnp.maximum(i * hb - 1, 0), 0)),
            pl.BlockSpec((halo, D), lambda i: (jnp.minimum((i + 1) * hb, n_h - 1), 0)),
            pl.BlockSpec((CONV_WIDTH, D), const2),
            pl.BlockSpec((1, D), const2),
            pl.BlockSpec((1, D), const2),
            pl.BlockSpec((1, D), const2),
            pl.BlockSpec((tm, D), tok),
            pl.BlockSpec((None, N_MOD, D), lambda i: (i // ss_per, 0, 0)),
            pl.BlockSpec((tm, D), tok),
            pl.BlockSpec((tm, LORA_G), lambda i: (i, gd_blk)),
            pl.BlockSpec((tm, D), tok),
            pl.BlockSpec((tm, D), tok),
            pl.BlockSpec((LORA_G, D), const2),
            pl.BlockSpec((D, D), const2),
        ],
        out_specs=pl.BlockSpec((tm, D), tok),
        scratch_shapes=[pltpu.VMEM((SUBLANES, tm + 2 * halo, D), F32), pltpu.VMEM((tm, D), F32)],
        compiler_params=_cparams(("parallel",)),
        name="conv_mixout",
    )(z, z, z, w, b, lnw, lnb, x, modss, y_att, u, ga, gb, g_up, w_out)


def _grid_pos_embed(rows):
    n_freq = D // 4
    freqs = jnp.exp(-math.log(POS_MAX_PERIOD) * jnp.arange(n_freq, dtype=F32) / n_freq)
    ang_r = jnp.arange(rows, dtype=F32)[:, None] * freqs
    ang_c = jnp.arange(GRID_W, dtype=F32)[:, None] * freqs
    emb_r = jnp.concatenate([jnp.sin(ang_r), jnp.cos(ang_r)], axis=-1)
    emb_c = jnp.concatenate([jnp.sin(ang_c), jnp.cos(ang_c)], axis=-1)
    return jnp.concatenate([jnp.repeat(emb_r, GRID_W, axis=0), jnp.tile(emb_c, (rows, 1))], axis=-1)


def kernel(x_prompt, x_sample, state_rwkv_fwd, state_rwkv_bwd, c, c_ctx, w_mod, b_mod, norm_ffn1, ffn1_w_in, ffn1_w_out, norm_mix, mix_w_in, rwkv_mu, rwkv_w0, rwkv_w_up, rwkv_a0, rwkv_a_up, rwkv_g_up, rwkv_k_k, rwkv_k_a, rwkv_r_k, rwkv_lnx_w, rwkv_lnx_b, conv_dw, conv_db, conv_ln_w, conv_ln_b, mix_w_out, norm_ffn2, ffn2_w_in, ffn2_w_out, norm_final):
    assert x_prompt.shape == (CTX_B, CTX_T, D) and x_sample.shape == (LAT_B, LAT_T, D)
    assert w_mod.shape[0] == 1, "single-layer trunk"
    row = lambda t: t.reshape(1, -1)

    cond = jnp.concatenate([c, c_ctx[None], jnp.zeros((8 - LAT_B - 1, D), F32)], axis=0)
    mod = _mod_call(cond, w_mod[0], row(b_mod[0])).reshape(8, N_MOD, D)
    ss_cond = jnp.array([LAT_B] * N_CTX_SS + list(range(LAT_B)), jnp.int32)
    modss = mod[ss_cond]

    pos = _grid_pos_embed(LAT_T // GRID_W).astype(x_sample.dtype)

    w_mix = mix_w_in[0]
    glu_off = N_RWKV_COLS
    gate_off = N_RWKV_COLS + 2 * D
    col_pad = ((0, 0), (0, N_RWKV_PAD - N_RWKV_COLS))
    w_glu = w_mix[:, glu_off:gate_off].astype(BF16)
    w_gate = w_mix[:, gate_off:].astype(BF16)

    x1 = _ffn_first_call(x_prompt.reshape(N_CTX, D), x_sample.reshape(LAT_B * LAT_T, D), pos, modss,
                         row(norm_ffn1[0]), ffn1_w_in[0], ffn1_w_out[0])

    u = _mixin_rwkv_call(x1, modss, row(norm_mix[0]), w_mix, jnp.pad(row(rwkv_mu[0]), col_pad))
    z, ga, gb = _mixin_gate_call(x1, modss, row(norm_mix[0]), w_glu, w_gate)

    ii = lax.broadcasted_iota(jnp.int32, (STAT_W, STAT_W), 0) // HEAD_DIM
    jj = lax.broadcasted_iota(jnp.int32, (STAT_W, STAT_W), 1) // HEAD_DIM
    bd = jnp.where(ii == jj, 1.0, 0.0).astype(BF16)

    y_att, s_f, s_b = _scan_call(
        u, state_rwkv_fwd[:, 0], state_rwkv_bwd[:, 0],
        rwkv_w0[0].reshape(2, 1, D), rwkv_w_up[0].astype(BF16), rwkv_a0[0].reshape(2, 1, D),
        rwkv_a_up[0].astype(BF16), row(rwkv_k_k[0]), row(rwkv_k_a[0]), row(rwkv_r_k[0]), bd,
        row(rwkv_lnx_w[0]), row(rwkv_lnx_b[0]))

    x2 = _convmix_call(z, conv_dw[0], row(conv_db[0]), row(conv_ln_w[0]), row(conv_ln_b[0]),
                       x1, modss, y_att, u, ga, gb, rwkv_g_up[0].astype(BF16), mix_w_out[0].astype(BF16))

    y_ctx, y_lat = _ffn_last_call(x2, modss, row(norm_ffn2[0]), ffn2_w_in[0], ffn2_w_out[0], row(norm_final))

    st = (CTX_B, 1, N_HEADS, HEAD_DIM, HEAD_DIM)
    return (y_ctx.reshape(CTX_B, CTX_T, D), y_lat.reshape(LAT_B, LAT_T, D),
            s_f.reshape(st).astype(x_prompt.dtype), s_b.reshape(st).astype(x_prompt.dtype))
```
